```python
import math
import jax
import jax.numpy as jnp
from jax import lax
import numpy as np


D_MODEL = 1024
BATCH = 16
SEQ = 4096
DEPTH = 2
DEC_BATCH = 8
DEC_SEQ = 4096
PAST_LEN = 128

HEAD_DIM = 64
A_HEADS = 8
A_KV_HEADS = 2
A_WINDOW = 128
A_BLOCK = 128
B_HEADS = 4
B_V_DIM = 2 * HEAD_DIM
C_HEADS = 8
C_Q_RANK = 384
C_KV_RANK = 256
C_NOPE = 64
C_ROPE = 32
C_V = 64
D_HEADS = 8
D_PATTERNS = ((128, 1), (512, 4), (2048, 16))
FFN_DIM = 2816
N_EXPERTS = 8
TOP_K = 2
EXPERT_DIM = 3584
Q_BLOCK = 128
MOE_CHUNK = 2048
ROPE_THETA = 10000.0
RMS_EPS = 1e-6
NEG_INF = -1e30
N_AB_LAYERS = (DEPTH + 1) // 2
N_CD_LAYERS = DEPTH // 2
AB_WIDTHS = (A_HEADS * HEAD_DIM, A_KV_HEADS * HEAD_DIM, A_KV_HEADS * HEAD_DIM,
             B_HEADS * 2 * HEAD_DIM, B_HEADS * 2 * HEAD_DIM, B_HEADS * B_V_DIM)
AB_IN = sum(AB_WIDTHS)
AB_OUT = A_HEADS * HEAD_DIM + B_HEADS * B_V_DIM
CD_WIDTHS = (C_Q_RANK, C_KV_RANK, C_ROPE,
             D_HEADS * HEAD_DIM, D_HEADS * HEAD_DIM, D_HEADS * HEAD_DIM)
CD_IN = sum(CD_WIDTHS)
CD_OUT = C_HEADS * C_V + D_HEADS * HEAD_DIM

kernel_name = 'hybrid_bidir_encoder_two_batches'


def _split(x, widths):
    outs, off = [], 0
    for w in widths:
        outs.append(x[..., off:off + w])
        off += w
    return outs


def _alibi_slopes(n):
    return (2.0 ** (-8.0 * (np.arange(n) + 1) / n)).astype(np.float32)


def _rmsnorm(x, g):
    xf = x.astype(jnp.float32)
    y = xf * lax.rsqrt(jnp.mean(xf * xf, axis=-1, keepdims=True) + RMS_EPS)
    return (y * g.astype(jnp.float32)).astype(x.dtype)


def _rope(x, pos):
    r = x.shape[-1]
    half = r // 2
    inv = jnp.asarray((ROPE_THETA ** (-np.arange(half) * 2.0 / r)).astype(np.float32))
    ang = pos[:, None] * inv[None, :]
    cos, sin = jnp.cos(ang)[:, None, :], jnp.sin(ang)[:, None, :]
    xf = x.astype(jnp.float32)
    x1, x2 = xf[..., :half], xf[..., half:]
    return jnp.concatenate([x1 * cos - x2 * sin, x1 * sin + x2 * cos], axis=-1).astype(x.dtype)


def _banded_attention(q, k, v, radius, blk, slopes, key_valid, dist_scale):
    b, t, h, hd = q.shape
    hkv = k.shape[2]
    g = h // hkv
    nb = t // blk

    def neighbours(a):
        ap = jnp.pad(a, [(0, 0), (blk, blk)] + [(0, 0)] * (a.ndim - 2))
        ap = ap.reshape((a.shape[0], nb + 2, blk) + a.shape[2:])
        return jnp.concatenate([ap[:, :-2], ap[:, 1:-1], ap[:, 2:]], axis=2)

    qb = q.reshape(b, nb, blk, hkv, g, hd)
    kb, vb, valid = neighbours(k), neighbours(v), neighbours(key_valid)
    s = jnp.einsum('bnqkgd,bnskd->bnkgqs', qb, kb, preferred_element_type=jnp.float32) * hd ** -0.5
    dist = np.abs(np.arange(3 * blk)[None, :] - blk - np.arange(blk)[:, None])
    bias = jnp.asarray(slopes.reshape(hkv, g, 1, 1) * (dist * dist_scale).astype(np.float32))
    mask = jnp.asarray(dist <= radius) & valid[:, :, None, None, None, :]
    s = jnp.where(mask, s - bias, NEG_INF)
    lse = jax.nn.logsumexp(s, axis=-1)
    p = jnp.exp(s - lse[..., None])
    o = jnp.einsum('bnkgqs,bnskd->bnqkgd', p.astype(v.dtype), vb).reshape(b, t, h, hd)
    lse = lse.transpose(0, 1, 4, 2, 3).reshape(b, t, h)
    return o, lse


def _diff_attention(q1, q2, k1, k2, v, lam, slopes, blk):
    b, t, h, hd = q1.shape
    nb = t // blk
    qq = jnp.stack([q1, q2], 0).reshape(2, b, nb, blk, h, hd).transpose(2, 0, 1, 3, 4, 5)
    kk = jnp.stack([k1, k2], 0)
    slopes = jnp.asarray(slopes)
    kpos = jnp.arange(t)

    def one_block(args):
        qblk, i = args
        s = jnp.einsum('cbqhd,cbshd->cbhqs', qblk, kk, preferred_element_type=jnp.float32) * hd ** -0.5
        qpos = i * blk + jnp.arange(blk)
        dist = jnp.abs(qpos[:, None] - kpos[None, :]).astype(jnp.float32)
        p = jax.nn.softmax(s - slopes[:, None, None] * dist, axis=-1)
        a = p[0] - lam * p[1]
        return jnp.einsum('bhqs,bshe->bqhe', a.astype(v.dtype), v)

    o = lax.map(one_block, (qq, jnp.arange(nb)))
    return o.transpose(1, 0, 2, 3, 4).reshape(b, t, h, v.shape[-1])


def _dense_attention(q, k, v, blk):
    b, t, h, dq = q.shape
    nb = t // blk
    qb = q.reshape(b, nb, blk, h, dq).transpose(1, 0, 2, 3, 4)

    def one_block(qblk):
        s = jnp.einsum('bqhd,bshd->bhqs', qblk, k, preferred_element_type=jnp.float32) * dq ** -0.5
        p = jax.nn.softmax(s, axis=-1)
        return jnp.einsum('bhqs,bshe->bqhe', p.astype(v.dtype), v)

    o = lax.map(one_block, qb)
    return o.transpose(1, 0, 2, 3, 4).reshape(b, t, h, v.shape[-1])


def _dilated_branch(q, k, v, window, dil, slopes):
    b, t, h, hd = q.shape
    radius = window // (2 * dil)
    span = dil * radius
    tp = -(-t // span) * span
    tsub = tp // dil

    def to_residue(a):
        a = jnp.pad(a, ((0, 0), (0, tp - t), (0, 0), (0, 0)))
        return a.reshape(b, tsub, dil, h, hd).transpose(0, 2, 1, 3, 4).reshape(b * dil, tsub, h, hd)

    valid = (np.arange(tp) < t).reshape(tsub, dil).T
    valid = jnp.broadcast_to(jnp.asarray(valid)[None], (b, dil, tsub)).reshape(b * dil, tsub)
    o, lse = _banded_attention(to_residue(q), to_residue(k), to_residue(v), radius, radius,
                               slopes, valid, dil)
    o = o.reshape(b, dil, tsub, h, hd).transpose(0, 2, 1, 3, 4).reshape(b, tp, h, hd)[:, :t]
    lse = lse.reshape(b, dil, tsub, h).transpose(0, 2, 1, 3).reshape(b, tp, h)[:, :t]
    return o, lse


def _mixer_ab(h, w_in, sink, lq1, lk1, lq2, lk2, subln, w_out, lam_init):
    b, t, _ = h.shape
    qa, ka, va, qd, kd, vd = _split(h @ w_in, AB_WIDTHS)
    qa = qa.reshape(b, t, A_HEADS, HEAD_DIM)
    ka = ka.reshape(b, t, A_KV_HEADS, HEAD_DIM)
    va = va.reshape(b, t, A_KV_HEADS, HEAD_DIM)
    oa, lse = _banded_attention(qa, ka, va, A_WINDOW, A_BLOCK, _alibi_slopes(A_HEADS),
                                jnp.ones((1, t), dtype=bool), 1)
    oa = (oa * jax.nn.sigmoid(lse - sink.astype(jnp.float32))[..., None]).astype(h.dtype)
    qd = qd.reshape(b, t, B_HEADS, 2, HEAD_DIM)
    kd = kd.reshape(b, t, B_HEADS, 2, HEAD_DIM)
    vd = vd.reshape(b, t, B_HEADS, B_V_DIM)
    f32 = jnp.float32
    lam = (jnp.exp(jnp.sum(lq1.astype(f32) * lk1.astype(f32)))
           - jnp.exp(jnp.sum(lq2.astype(f32) * lk2.astype(f32))) + lam_init)
    od = _diff_attention(qd[..., 0, :], qd[..., 1, :], kd[..., 0, :], kd[..., 1, :], vd, lam,
                         _alibi_slopes(B_HEADS), Q_BLOCK)
    od = _rmsnorm(od, subln) * (1.0 - lam_init)
    o = jnp.concatenate([oa.reshape(b, t, -1), od.reshape(b, t, -1)], axis=-1)
    return o @ w_out


def _mixer_cd(h, w_in, q_norm, kv_norm, w_uq, w_ukv, w_out):
    b, t, _ = h.shape
    cq, ckv, kr, qd, kd, vd = _split(h @ w_in, CD_WIDTHS)
    q = (_rmsnorm(cq, q_norm) @ w_uq).reshape(b, t, C_HEADS, C_NOPE + C_ROPE)
    kv = (_rmsnorm(ckv, kv_norm) @ w_ukv).reshape(b, t, C_HEADS, C_NOPE + C_V)
    pos = jnp.arange(t, dtype=jnp.float32)
    q_rope = _rope(q[..., C_NOPE:], pos)
    k_rope = jnp.broadcast_to(_rope(kr.reshape(b, t, 1, C_ROPE), pos), (b, t, C_HEADS, C_ROPE))
    qc = jnp.concatenate([q[..., :C_NOPE], q_rope], axis=-1)
    kc = jnp.concatenate([kv[..., :C_NOPE], k_rope], axis=-1)
    oc = _dense_attention(qc, kc, kv[..., C_NOPE:], Q_BLOCK)
    qd = qd.reshape(b, t, D_HEADS, HEAD_DIM)
    kd = kd.reshape(b, t, D_HEADS, HEAD_DIM)
    vd = vd.reshape(b, t, D_HEADS, HEAD_DIM)
    slopes = _alibi_slopes(D_HEADS)
    outs, lses = zip(*[_dilated_branch(qd, kd, vd, w, r, slopes) for (w, r) in D_PATTERNS])
    alpha = jax.nn.softmax(jnp.stack(lses, 0), axis=0)
    od = jnp.einsum('gbth,gbthd->bthd', alpha.astype(h.dtype), jnp.stack(outs, 0))
    o = jnp.concatenate([oc.reshape(b, t, -1), od.reshape(b, t, -1)], axis=-1)
    return o @ w_out


def _swiglu(h, wg, wu, wd):
    return (jax.nn.silu(h @ wg) * (h @ wu)) @ wd


def _moe_swiglu(h, router, wg, wu, wd):
    b, t, dm = h.shape
    n = b * t
    x = h.reshape(n, dm)
    logits = jnp.einsum('nd,de->ne', x, router, preferred_element_type=jnp.float32)
    top_v, top_i = lax.top_k(logits, TOP_K)
    gates = jax.nn.softmax(top_v, axis=-1)
    gate_full = jnp.einsum('nk,nke->ne', gates, jax.nn.one_hot(top_i, N_EXPERTS, dtype=jnp.float32))
    chunk = math.gcd(n, MOE_CHUNK)

    def one_chunk(args):
        xb, gb = args
        hg = jnp.einsum('nd,edf->enf', xb, wg)
        hu = jnp.einsum('nd,edf->enf', xb, wu)
        y = jnp.einsum('enf,efd->end', jax.nn.silu(hg) * hu, wd)
        return jnp.einsum('ne,end->nd', gb.astype(y.dtype), y)

    y = lax.map(one_chunk, (x.reshape(n // chunk, chunk, dm),
                            gate_full.reshape(n // chunk, chunk, N_EXPERTS)))
    return y.reshape(b, t, dm)


def _trunk(x, p):
    for layer in range(DEPTH):
        i = layer // 2
        if layer % 2 == 0:
            lam_init = 0.8 - 0.6 * math.exp(-0.3 * layer)
            x = x + _mixer_ab(_rmsnorm(x, p['ab_norm_attn'][i]), p['ab_w_in'][i], p['ab_sink'][i],
                              p['ab_lambda_q1'][i], p['ab_lambda_k1'][i], p['ab_lambda_q2'][i],
                              p['ab_lambda_k2'][i], p['ab_subln'][i], p['ab_w_out'][i], lam_init)
            x = x + _swiglu(_rmsnorm(x, p['ab_norm_ffn'][i]), p['ffn_w_gate'][i], p['ffn_w_up'][i],
                            p['ffn_w_down'][i])
        else:
            x = x + _mixer_cd(_rmsnorm(x, p['cd_norm_attn'][i]), p['cd_w_in'][i], p['cd_q_norm'][i],
                              p['cd_kv_norm'][i], p['cd_w_uq'][i], p['cd_w_ukv'][i], p['cd_w_out'][i])
            x = x + _moe_swiglu(_rmsnorm(x, p['cd_norm_ffn'][i]), p['moe_router'][i],
                                p['moe_w_gate'][i], p['moe_w_up'][i], p['moe_w_down'][i])
    return _rmsnorm(x, p['final_norm'])


def setup_inputs(seed: int = 0) -> dict:
    key = jax.random.key(seed)
    ks = jax.random.split(key, 28)

    def nrm(k, shape, scale):
        return scale * jax.random.normal(k, shape, jnp.float32)

    def gain(k, shape):
        return 1.0 + nrm(k, shape, 0.02)

    l0, l1 = N_AB_LAYERS, N_CD_LAYERS
    return {
        'x_prompt': nrm(ks[0], (BATCH, SEQ, D_MODEL), 1.0),
        'x_sample': nrm(ks[1], (DEC_BATCH, DEC_SEQ, D_MODEL), 1.0),
        'ab_norm_attn': gain(ks[2], (l0, D_MODEL)),
        'ab_w_in': nrm(ks[3], (l0, D_MODEL, AB_IN), D_MODEL ** -0.5),
        'ab_sink': nrm(ks[4], (l0, A_HEADS), 0.5),
        'ab_lambda_q1': nrm(ks[5], (l0, HEAD_DIM), 0.1),
        'ab_lambda_k1': nrm(ks[6], (l0, HEAD_DIM), 0.1),
        'ab_lambda_q2': nrm(ks[7], (l0, HEAD_DIM), 0.1),
        'ab_lambda_k2': nrm(ks[8], (l0, HEAD_DIM), 0.1),
        'ab_subln': gain(ks[9], (l0, B_V_DIM)),
        'ab_w_out': nrm(ks[10], (l0, AB_OUT, D_MODEL), AB_OUT ** -0.5),
        'ab_norm_ffn': gain(ks[11], (l0, D_MODEL)),
        'ffn_w_gate': nrm(ks[12], (l0, D_MODEL, FFN_DIM), D_MODEL ** -0.5),
        'ffn_w_up': nrm(ks[13], (l0, D_MODEL, FFN_DIM), D_MODEL ** -0.5),
        'ffn_w_down': nrm(ks[14], (l0, FFN_DIM, D_MODEL), FFN_DIM ** -0.5),
        'cd_norm_attn': gain(ks[15], (l1, D_MODEL)),
        'cd_w_in': nrm(ks[16], (l1, D_MODEL, CD_IN), D_MODEL ** -0.5),
        'cd_q_norm': gain(ks[17], (l1, C_Q_RANK)),
        'cd_kv_norm': gain(ks[18], (l1, C_KV_RANK)),
        'cd_w_uq': nrm(ks[19], (l1, C_Q_RANK, C_HEADS * (C_NOPE + C_ROPE)), C_Q_RANK ** -0.5),
        'cd_w_ukv': nrm(ks[20], (l1, C_KV_RANK, C_HEADS * (C_NOPE + C_V)), C_KV_RANK ** -0.5),
        'cd_w_out': nrm(ks[21], (l1, CD_OUT, D_MODEL), CD_OUT ** -0.5),
        'cd_norm_ffn': gain(ks[22], (l1, D_MODEL)),
        'moe_router': nrm(ks[23], (l1, D_MODEL, N_EXPERTS), D_MODEL ** -0.5),
        'moe_w_gate': nrm(ks[24], (l1, N_EXPERTS, D_MODEL, EXPERT_DIM), D_MODEL ** -0.5),
        'moe_w_up': nrm(ks[25], (l1, N_EXPERTS, D_MODEL, EXPERT_DIM), D_MODEL ** -0.5),
        'moe_w_down': nrm(ks[26], (l1, N_EXPERTS, EXPERT_DIM, D_MODEL), EXPERT_DIM ** -0.5),
        'final_norm': gain(ks[27], (D_MODEL,)),
    }


def reference(x_prompt, x_sample, ab_norm_attn, ab_w_in, ab_sink, ab_lambda_q1, ab_lambda_k1,
              ab_lambda_q2, ab_lambda_k2, ab_subln, ab_w_out, ab_norm_ffn, ffn_w_gate, ffn_w_up,
              ffn_w_down, cd_norm_attn, cd_w_in, cd_q_norm, cd_kv_norm, cd_w_uq, cd_w_ukv, cd_w_out,
              cd_norm_ffn, moe_router, moe_w_gate, moe_w_up, moe_w_down, final_norm):
    p = dict(ab_norm_attn=ab_norm_attn, ab_w_in=ab_w_in, ab_sink=ab_sink,
             ab_lambda_q1=ab_lambda_q1, ab_lambda_k1=ab_lambda_k1, ab_lambda_q2=ab_lambda_q2,
             ab_lambda_k2=ab_lambda_k2, ab_subln=ab_subln, ab_w_out=ab_w_out,
             ab_norm_ffn=ab_norm_ffn, ffn_w_gate=ffn_w_gate, ffn_w_up=ffn_w_up,
             ffn_w_down=ffn_w_down, cd_norm_attn=cd_norm_attn, cd_w_in=cd_w_in,
             cd_q_norm=cd_q_norm, cd_kv_norm=cd_kv_norm, cd_w_uq=cd_w_uq, cd_w_ukv=cd_w_ukv,
             cd_w_out=cd_w_out, cd_norm_ffn=cd_norm_ffn, moe_router=moe_router,
             moe_w_gate=moe_w_gate, moe_w_up=moe_w_up, moe_w_down=moe_w_down,
             final_norm=final_norm)
    y_prompt = _trunk(x_prompt, p)
    y_sample = _trunk(x_sample, p)
    return (y_prompt, y_sample)
```

```python
import functools
import math

import numpy as np
import jax
import jax.numpy as jnp
from jax import lax
from jax.experimental import pallas as pl
from jax.experimental.pallas import tpu as pltpu

F32 = jnp.float32
BF16 = jnp.bfloat16

D_MODEL = 1024
HEAD_DIM = 64
A_HEADS = 8
A_KV_HEADS = 2
A_WINDOW = 128
A_BLOCK = 128
B_HEADS = 4
B_V_DIM = 2 * HEAD_DIM
C_HEADS = 8
C_Q_RANK = 384
C_KV_RANK = 256
C_NOPE = 64
C_ROPE = 32
C_V = 64
D_HEADS = 8
D_PATTERNS = ((128, 1), (512, 4), (2048, 16))
FFN_DIM = 2816
N_EXPERTS = 8
EXPERT_DIM = 3584
ROPE_THETA = 10000.0
RMS_EPS = 1e-6
NEG_INF = -1e30
LANES = 128
VMEM_LIMIT = 56 * 1024 * 1024

AB_WIDTHS = (A_HEADS * HEAD_DIM, A_KV_HEADS * HEAD_DIM, A_KV_HEADS * HEAD_DIM,
             B_HEADS * 2 * HEAD_DIM, B_HEADS * 2 * HEAD_DIM, B_HEADS * B_V_DIM)
CD_WIDTHS = (C_Q_RANK, C_KV_RANK, C_ROPE,
             D_HEADS * HEAD_DIM, D_HEADS * HEAD_DIM, D_HEADS * HEAD_DIM)
CD_PROJ_WIDTHS = (C_Q_RANK, C_KV_RANK, LANES, LANES,
                  D_HEADS * HEAD_DIM, D_HEADS * HEAD_DIM, D_HEADS * HEAD_DIM)

ROW_TILE = 512
MOE_TILE = 1024
MOE_FCHUNK = 512
GATHER_CHUNK = 1024
FFN_CHUNK = 256


def _alibi_slopes(n):
    return [float(2.0 ** (-8.0 * (i + 1) / n)) for i in range(n)]


def _params(sem, vmem=VMEM_LIMIT):
    return pltpu.CompilerParams(dimension_semantics=sem, vmem_limit_bytes=vmem)


def _const_spec(a):
    return pl.BlockSpec(a.shape, lambda *_: (0,) * a.ndim, pipeline_mode=pl.Buffered(1))


def _rms(x, g):
    ms = jnp.mean(x * x, axis=-1, keepdims=True)
    return x * lax.rsqrt(ms + RMS_EPS) * g


def _dot(a, b):
    return jnp.dot(a, b, preferred_element_type=F32)


def _dot_nt(a, b):
    return lax.dot_general(a, b, (((1,), (1,)), ((), ())), preferred_element_type=F32)


def _norm_proj_kernel(x_ref, g_ref, w_ref, *o_refs, widths, scales):
    xn = _rms(x_ref[...], g_ref[...]).astype(BF16)
    off = 0
    for o_ref, w, sc in zip(o_refs, widths, scales):
        y = _dot(xn, w_ref[:, off:off + w])
        if sc != 1.0:
            y = y * sc
        o_ref[...] = y.astype(o_ref.dtype)
        off += w


def _norm_proj(x, g, w, widths, scales):
    n, d = x.shape
    tm = min(ROW_TILE, n)
    return pl.pallas_call(
        functools.partial(_norm_proj_kernel, widths=widths, scales=scales),
        grid=(n // tm,),
        in_specs=[pl.BlockSpec((tm, d), lambda i: (i, 0)),
                  pl.BlockSpec((1, d), lambda i: (0, 0)),
                  pl.BlockSpec(w.shape, lambda i: (0, 0))],
        out_specs=[pl.BlockSpec((tm, wd), lambda i: (i, 0)) for wd in widths],
        out_shape=[jax.ShapeDtypeStruct((n, wd), BF16) for wd in widths],
        compiler_params=_params(("parallel",)),
        name="norm_proj",
    )(x, g.reshape(1, d), w)


def _attn_a_kernel(sink_ref, q_ref, kp_ref, kc_ref, kn_ref, vp_ref, vc_ref, vn_ref, o_ref, *, nb):
    i = pl.program_id(1)
    blk = A_BLOCK
    kcat = jnp.concatenate([kp_ref[...], kc_ref[...], kn_ref[...]], axis=0)
    vcat = jnp.concatenate([vp_ref[...], vc_ref[...], vn_ref[...]], axis=0)
    row = lax.broadcasted_iota(jnp.int32, (blk, 3 * blk), 0)
    col = lax.broadcasted_iota(jnp.int32, (blk, 3 * blk), 1)
    dist = jnp.abs(col - blk - row)
    valid = (dist <= A_WINDOW) & ((col >= blk) | (i > 0)) & ((col < 2 * blk) | (i < nb - 1))
    distf = dist.astype(F32)
    lane = lax.broadcasted_iota(jnp.int32, (blk, LANES), 1)
    low = lane < HEAD_DIM
    slopes = _alibi_slopes(A_HEADS)
    half_heads = A_HEADS // 2
    for p in range(half_heads):
        qp = q_ref[:, LANES * p:LANES * (p + 1)]
        res = []
        for half in range(2):
            h = p + half_heads * half
            qm = jnp.where(low if half == 0 else jnp.logical_not(low), qp, jnp.zeros_like(qp))
            s = _dot_nt(qm, kcat)
            s = jnp.where(valid, s - slopes[h] * distf, NEG_INF)
            m = jnp.max(s, axis=-1, keepdims=True)
            e = jnp.exp(s - m)
            l = jnp.sum(e, axis=-1, keepdims=True)
            lse = m + jnp.log(l)
            gate = 1.0 / (1.0 + jnp.exp(sink_ref[h] - lse))
            o = _dot(e.astype(BF16), vcat)
            res.append(o * (gate / l))
        o_ref[:, LANES * p:LANES * (p + 1)] = jnp.where(low, res[0], res[1]).astype(o_ref.dtype)


def _attn_a(sink, qa, ka, va, batch, t):
    nb = t // A_BLOCK
    qw = qa.shape[1]
    kw = ka.shape[1]

    def nbr(delta):
        return lambda b, i: (b * nb + jnp.clip(i + delta, 0, nb - 1), 0)

    kv_specs = [pl.BlockSpec((A_BLOCK, kw), nbr(d)) for d in (-1, 0, 1)]
    return pl.pallas_call(
        functools.partial(_attn_a_kernel, nb=nb),
        grid=(batch, nb),
        in_specs=[pl.BlockSpec(memory_space=pltpu.SMEM),
                  pl.BlockSpec((A_BLOCK, qw), lambda b, i: (b * nb + i, 0))] + kv_specs + kv_specs,
        out_specs=pl.BlockSpec((A_BLOCK, qw), lambda b, i: (b * nb + i, 0)),
        out_shape=jax.ShapeDtypeStruct(qa.shape, BF16),
        compiler_params=_params(("parallel", "parallel")),
        name="attn_a",
    )(sink, qa, ka, ka, ka, va, va, va)


def _softmax_step(s, v, m_ref, l_ref, acc_ref, weight=None):
    m_old = m_ref[...]
    m_new = jnp.maximum(m_old, jnp.max(s, axis=-1, keepdims=True))
    alpha = jnp.exp(m_old - m_new)
    p = jnp.exp(s - m_new)
    if weight is not None:
        p = p * weight
    l_ref[...] = alpha * l_ref[...] + jnp.sum(p, axis=-1, keepdims=True)
    acc_ref[...] = alpha * acc_ref[...] + _dot(p.astype(BF16), v)
    m_ref[...] = m_new


def _softmax_init(m_ref, l_ref, acc_ref):
    m_ref[...] = jnp.full(m_ref.shape, NEG_INF, F32)
    l_ref[...] = jnp.zeros(l_ref.shape, F32)
    acc_ref[...] = jnp.zeros(acc_ref.shape, F32)


def _attn_b_kernel(lq1_ref, lk1_ref, lq2_ref, lk2_ref, subln_ref, q_ref, k_ref, v_ref, o_ref,
                   m_ref, l_ref, acc_ref, *, lam_init, t, tq, tk):
    i = pl.program_id(1)
    lam = (jnp.exp(jnp.sum(lq1_ref[...] * lk1_ref[...], axis=-1, keepdims=True))
           - jnp.exp(jnp.sum(lq2_ref[...] * lk2_ref[...], axis=-1, keepdims=True)) + lam_init)
    lane = lax.broadcasted_iota(jnp.int32, (tq, LANES), 1)
    low = lane < HEAD_DIM
    rel = (lax.broadcasted_iota(jnp.int32, (tq, tk), 0) + i * tq
           - lax.broadcasted_iota(jnp.int32, (tq, tk), 1))
    slopes = _alibi_slopes(B_HEADS)
    for h in range(B_HEADS):
        cols = slice(LANES * h, LANES * (h + 1))
        qh = q_ref[:, cols]
        zero = jnp.zeros_like(qh)
        qq = jnp.concatenate([jnp.where(low, qh, zero), jnp.where(low, zero, qh)], axis=0)
        _softmax_init(m_ref, l_ref, acc_ref)

        def body(j, carry, cols=cols, qq=qq, slope=slopes[h]):
            kj = k_ref[pl.ds(pl.multiple_of(j * tk, tk), tk), cols]
            vj = v_ref[pl.ds(pl.multiple_of(j * tk, tk), tk), cols]
            bias = slope * jnp.abs(rel - j * tk).astype(F32)
            s = _dot_nt(qq, kj) - jnp.concatenate([bias, bias], axis=0)
            _softmax_step(s, vj, m_ref, l_ref, acc_ref)
            return carry

        lax.fori_loop(0, t // tk, body, 0)
        o = acc_ref[...] / l_ref[...]
        od = o[:tq] - lam * o[tq:]
        y = _rms(od, subln_ref[...]) * (1.0 - lam_init)
        o_ref[:, cols] = y.astype(o_ref.dtype)


def _attn_b(lq1, lk1, lq2, lk2, subln, qd, kd, vd, batch, t, lam_init):
    tq = min(256, t)
    tk = min(512, t)
    nq = t // tq
    w = qd.shape[1]
    small = lambda a: pl.BlockSpec((1, a.shape[-1]), lambda b, i: (0, 0))
    vecs = [a.reshape(1, -1) for a in (lq1, lk1, lq2, lk2, subln)]
    return pl.pallas_call(
        functools.partial(_attn_b_kernel, lam_init=lam_init, t=t, tq=tq, tk=tk),
        grid=(batch, nq),
        in_specs=[small(a) for a in vecs]
        + [pl.BlockSpec((tq, w), lambda b, i: (b * nq + i, 0)),
           pl.BlockSpec((t, w), lambda b, i: (b, 0)),
           pl.BlockSpec((t, w), lambda b, i: (b, 0))],
        out_specs=pl.BlockSpec((tq, w), lambda b, i: (b * nq + i, 0)),
        out_shape=jax.ShapeDtypeStruct(qd.shape, BF16),
        scratch_shapes=[pltpu.VMEM((2 * tq, 1), F32), pltpu.VMEM((2 * tq, 1), F32),
                        pltpu.VMEM((2 * tq, LANES), F32)],
        compiler_params=_params(("parallel", "arbitrary")),
        name="attn_b",
    )(*vecs, qd, kd, vd)


def _out_proj_kernel(x_ref, a1_ref, a2_ref, w1_ref, w2_ref, o_ref):
    o_ref[...] = x_ref[...] + _dot(a1_ref[...], w1_ref[...]) + _dot(a2_ref[...], w2_ref[...])


def _out_proj(x, a1, a2, w1, w2):
    n, d = x.shape
    tm = min(ROW_TILE, n)
    return pl.pallas_call(
        _out_proj_kernel,
        grid=(n // tm,),
        in_specs=[pl.BlockSpec((tm, d), lambda i: (i, 0)),
                  pl.BlockSpec((tm, a1.shape[1]), lambda i: (i, 0)),
                  pl.BlockSpec((tm, a2.shape[1]), lambda i: (i, 0)),
                  pl.BlockSpec(w1.shape, lambda i: (0, 0)),
                  pl.BlockSpec(w2.shape, lambda i: (0, 0))],
        out_specs=pl.BlockSpec((tm, d), lambda i: (i, 0)),
        out_shape=jax.ShapeDtypeStruct((n, d), F32),
        compiler_params=_params(("parallel",)),
        name="out_proj",
    )(x, a1, a2, w1, w2)


def _ffn_kernel(x_ref, g_ref, wg_ref, wu_ref, wd_ref, o_ref, acc_ref, *, nchunks, fc):
    x = x_ref[...]
    xn = _rms(x, g_ref[...]).astype(BF16)
    acc_ref[...] = x

    def body(c, carry):
        hg = _dot(xn, wg_ref[c])
        hu = _dot(xn, wu_ref[c])
        act = (hg / (1.0 + jnp.exp(-hg)) * hu).astype(BF16)
        acc_ref[...] += _dot(act, wd_ref[pl.ds(pl.multiple_of(c * fc, fc), fc), :])
        return carry

    lax.fori_loop(0, nchunks, body, 0)
    o_ref[...] = acc_ref[...]


def _ffn(x, g, wg3, wu3, wd):
    n, d = x.shape
    nchunks, _, fc = wg3.shape
    tm = min(ROW_TILE, n)
    return pl.pallas_call(
        functools.partial(_ffn_kernel, nchunks=nchunks, fc=fc),
        grid=(n // tm,),
        in_specs=[pl.BlockSpec((tm, d), lambda i: (i, 0)),
                  pl.BlockSpec((1, d), lambda i: (0, 0)),
                  _const_spec(wg3), _const_spec(wu3), _const_spec(wd)],
        out_specs=pl.BlockSpec((tm, d), lambda i: (i, 0)),
        out_shape=jax.ShapeDtypeStruct((n, d), F32),
        scratch_shapes=[pltpu.VMEM((tm, d), F32)],
        compiler_params=_params(("parallel",)),
        name="ffn",
    )(x, g.reshape(1, d), wg3, wu3, wd)


def _cd_proj_kernel(x_ref, g_ref, w_ref, qn_ref, kvn_ref, wq1_ref, wq2_ref, wk_ref, wv_ref,
                    cos_ref, sin_ref, qc_ref, kc_ref, vc_ref, qd_ref, kd_ref, vd_ref):
    h = _rms(x_ref[...], g_ref[...]).astype(BF16)
    offs = np.cumsum((0,) + CD_PROJ_WIDTHS)
    seg = lambda k: _dot(h, w_ref[:, int(offs[k]):int(offs[k + 1])])
    cq, ckv, kr_plain, kr_rot = seg(0), seg(1), seg(2), seg(3)
    qd_ref[...] = (seg(4) * HEAD_DIM ** -0.5).astype(BF16)
    kd_ref[...] = seg(5).astype(BF16)
    vd_ref[...] = seg(6).astype(BF16)
    cqn = _rms(cq, qn_ref[...]).astype(BF16)
    ckvn = _rms(ckv, kvn_ref[...]).astype(BF16)
    cos = cos_ref[...]
    sin = sin_ref[...]
    k_rope = kr_plain * cos + kr_rot * sin
    q1 = _dot(cqn, wq1_ref[...])
    q2 = _dot(cqn, wq2_ref[...])
    kn = _dot(ckvn, wk_ref[...])
    scale = (C_NOPE + C_ROPE) ** -0.5
    for hh in range(C_HEADS):
        cols = slice(LANES * hh, LANES * (hh + 1))
        qc_ref[:, cols] = ((q1[:, cols] * cos + q2[:, cols] * sin) * scale).astype(BF16)
        kc_ref[:, cols] = (kn[:, cols] + k_rope).astype(BF16)
    vc_ref[...] = _dot(ckvn, wv_ref[...]).astype(BF16)


def _cd_proj(x, g, w, qn, kvn, wq1, wq2, wk, wv, cos_t, sin_t, t):
    n, d = x.shape
    tm = min(ROW_TILE, t)
    nt = t // tm
    full = _const_spec
    consts = [g.reshape(1, d), w, qn.reshape(1, -1), kvn.reshape(1, -1), wq1, wq2, wk, wv]
    out_w = (C_HEADS * LANES, C_HEADS * LANES, C_HEADS * C_V) + CD_PROJ_WIDTHS[4:]
    return pl.pallas_call(
        _cd_proj_kernel,
        grid=(n // tm,),
        in_specs=[pl.BlockSpec((tm, d), lambda i: (i, 0))] + [full(a) for a in consts]
        + [pl.BlockSpec((tm, LANES), lambda i: (i % nt, 0)),
           pl.BlockSpec((tm, LANES), lambda i: (i % nt, 0))],
        out_specs=[pl.BlockSpec((tm, wd), lambda i: (i, 0)) for wd in out_w],
        out_shape=[jax.ShapeDtypeStruct((n, wd), BF16) for wd in out_w],
        compiler_params=_params(("parallel",)),
        name="cd_proj",
    )(x, *consts, cos_t, sin_t)


def _attn_c_kernel(q_ref, k_ref, v_ref, o_ref, m_ref, l_ref, acc_ref, *, t, tq, tk):
    q0 = q_ref[:, :LANES]
    q1 = q_ref[:, LANES:]
    _softmax_init(m_ref, l_ref, acc_ref)

    def body(j, carry):
        rows = pl.ds(pl.multiple_of(j * tk, tk), tk)
        s = jnp.concatenate([_dot_nt(q0, k_ref[rows, :LANES]), _dot_nt(q1, k_ref[rows, LANES:])],
                            axis=0)
        _softmax_step(s, v_ref[rows, :], m_ref, l_ref, acc_ref)
        return carry

    lax.fori_loop(0, t // tk, body, 0)
    o = acc_ref[...] / l_ref[...]
    lane = lax.broadcasted_iota(jnp.int32, (tq, LANES), 1)
    o_ref[...] = jnp.where(lane < C_V, o[:tq], o[tq:]).astype(o_ref.dtype)


def _attn_c(qc, kc, vc, batch, t):
    tq = min(256, t)
    tk = min(512, t)
    nq = t // tq
    pairs = C_HEADS // 2
    return pl.pallas_call(
        functools.partial(_attn_c_kernel, t=t, tq=tq, tk=tk),
        grid=(batch, pairs, nq),
        in_specs=[pl.BlockSpec((tq, 2 * LANES), lambda b, p, i: (b * nq + i, p)),
                  pl.BlockSpec((t, 2 * LANES), lambda b, p, i: (b, p)),
                  pl.BlockSpec((t, LANES), lambda b, p, i: (b, p))],
        out_specs=pl.BlockSpec((tq, LANES), lambda b, p, i: (b * nq + i, p)),
        out_shape=jax.ShapeDtypeStruct(vc.shape, BF16),
        scratch_shapes=[pltpu.VMEM((2 * tq, 1), F32), pltpu.VMEM((2 * tq, 1), F32),
                        pltpu.VMEM((2 * tq, LANES), F32)],
        compiler_params=_params(("parallel", "parallel", "arbitrary")),
        name="attn_c",
    )(qc, kc, vc)


def _attn_d_kernel(slope_ref, q_ref, k_ref, v_ref, o_ref, m_ref, l_ref, acc_ref, *, t, tq, tk):
    p = pl.program_id(1)
    i = pl.program_id(2)
    reach = max((w // (2 * dil)) * dil for w, dil in D_PATTERNS)
    q = q_ref[...]
    lane = lax.broadcasted_iota(jnp.int32, (tq, LANES), 1)
    low = lane < HEAD_DIM
    zero = jnp.zeros_like(q)
    qq = jnp.concatenate([jnp.where(low, q, zero), jnp.where(low, zero, q)], axis=0)
    s0 = slope_ref[2 * p]
    s1 = slope_ref[2 * p + 1]
    rel = (lax.broadcasted_iota(jnp.int32, (tq, tk), 1)
           - lax.broadcasted_iota(jnp.int32, (tq, tk), 0) - i * tq)
    _softmax_init(m_ref, l_ref, acc_ref)
    c_lo = jnp.maximum(i * tq - reach, 0) // tk
    c_hi = jnp.minimum((i * tq + tq + reach + tk - 1) // tk, t // tk)

    def body(c, carry):
        rows = pl.ds(pl.multiple_of(c * tk, tk), tk)
        d = rel + c * tk
        ad = jnp.abs(d)
        mult = jnp.zeros((tq, tk), F32)
        for w, dil in D_PATTERNS:
            hit = (ad <= (w // (2 * dil)) * dil) & ((d & (dil - 1)) == 0)
            mult = mult + jnp.where(hit, 1.0, 0.0)
        adf = ad.astype(F32)
        bias = jnp.concatenate([s0 * adf, s1 * adf], axis=0)
        mult2 = jnp.concatenate([mult, mult], axis=0)
        s = jnp.where(mult2 > 0.0, _dot_nt(qq, k_ref[rows, :]) - bias, NEG_INF)
        _softmax_step(s, v_ref[rows, :], m_ref, l_ref, acc_ref, weight=mult2)
        return carry

    lax.fori_loop(c_lo, c_hi, body, 0)
    o = acc_ref[...] / l_ref[...]
    o_ref[...] = jnp.where(low, o[:tq], o[tq:]).astype(o_ref.dtype)


def _attn_d(qd, kd, vd, batch, t):
    tq = min(128, t)
    tk = min(256, t)
    nq = t // tq
    pairs = D_HEADS // 2
    slopes = jnp.asarray(_alibi_slopes(D_HEADS), F32)
    return pl.pallas_call(
        functools.partial(_attn_d_kernel, t=t, tq=tq, tk=tk),
        grid=(batch, pairs, nq),
        in_specs=[pl.BlockSpec(memory_space=pltpu.SMEM),
                  pl.BlockSpec((tq, LANES), lambda b, p, i: (b * nq + i, p)),
                  pl.BlockSpec((t, LANES), lambda b, p, i: (b, p)),
                  pl.BlockSpec((t, LANES), lambda b, p, i: (b, p))],
        out_specs=pl.BlockSpec((tq, LANES), lambda b, p, i: (b * nq + i, p)),
        out_shape=jax.ShapeDtypeStruct(qd.shape, BF16),
        scratch_shapes=[pltpu.VMEM((2 * tq, 1), F32), pltpu.VMEM((2 * tq, 1), F32),
                        pltpu.VMEM((2 * tq, LANES), F32)],
        compiler_params=_params(("parallel", "parallel", "arbitrary")),
        name="attn_d",
    )(slopes, qd, kd, vd)


def _router_kernel(x_ref, g_ref, rt_ref, xn_ref, ids_ref, gates_ref):
    xn = _rms(x_ref[...], g_ref[...])
    xn_ref[...] = xn
    x_hi = xn.astype(BF16)
    x_lo = (xn - x_hi.astype(F32)).astype(BF16)
    rt = rt_ref[...]
    r_hi = rt.astype(BF16)
    r_lo = (rt - r_hi.astype(F32)).astype(BF16)
    logits = _dot_nt(r_hi, x_hi) + _dot_nt(r_hi, x_lo) + _dot_nt(r_lo, x_hi)
    eid = lax.broadcasted_iota(jnp.int32, logits.shape, 0)
    v1 = jnp.max(logits, axis=0, keepdims=True)
    i1 = jnp.min(jnp.where(logits == v1, eid, N_EXPERTS), axis=0, keepdims=True)
    rest = jnp.where(eid == i1, -jnp.inf, logits)
    v2 = jnp.max(rest, axis=0, keepdims=True)
    i2 = jnp.min(jnp.where(rest == v2, eid, N_EXPERTS), axis=0, keepdims=True)
    e2 = jnp.exp(v2 - v1)
    g1 = 1.0 / (1.0 + e2)
    ids_ref[...] = jnp.concatenate([i1, i2], axis=0)
    gates_ref[...] = jnp.concatenate([g1, e2 * g1], axis=0)


def _router(x, g, router_t):
    n, d = x.shape
    tm = min(ROW_TILE, n)
    return pl.pallas_call(
        _router_kernel,
        grid=(n // tm,),
        in_specs=[pl.BlockSpec((tm, d), lambda i: (i, 0)),
                  pl.BlockSpec((1, d), lambda i: (0, 0)),
                  pl.BlockSpec(router_t.shape, lambda i: (0, 0))],
        out_specs=[pl.BlockSpec((tm, d), lambda i: (i, 0)),
                   pl.BlockSpec((2, tm), lambda i: (0, i)),
                   pl.BlockSpec((2, tm), lambda i: (0, i))],
        out_shape=[jax.ShapeDtypeStruct((n, d), F32),
                   jax.ShapeDtypeStruct((2, n), jnp.int32),
                   jax.ShapeDtypeStruct((2, n), F32)],
        compiler_params=_params(("parallel",)),
        name="router",
    )(x, g.reshape(1, d), router_t)


def _gather_kernel(idx_hbm, x_hbm, o_hbm, idx_a, idx_b, isem, rsem, *, nchunks):
    rc = GATHER_CHUNK
    slots = (idx_a, idx_b)

    def idx_copy(c, slot):
        return pltpu.make_async_copy(idx_hbm.at[pl.ds(pl.multiple_of(c * rc, rc), rc)],
                                     slots[slot], isem.at[slot])

    def row_copy(tok, dst, slot):
        return pltpu.make_async_copy(x_hbm.at[pl.ds(tok, 1)], o_hbm.at[pl.ds(dst, 1)], rsem.at[slot])

    def issue(c, slot):
        def row(r, carry):
            row_copy(slots[slot][r], c * rc + r, slot).start()
            return carry
        lax.fori_loop(0, rc, row, 0)

    def drain(slot):
        def row(r, carry):
            row_copy(0, 0, slot).wait()
            return carry
        lax.fori_loop(0, rc, row, 0)

    idx_copy(0, 0).start()

    def pair(c2, carry):
        c = 2 * c2
        idx_copy(c, 0).wait()
        idx_copy(c + 1, 1).start()
        issue(c, 0)

        @pl.when(c2 > 0)
        def _():
            drain(1)

        idx_copy(c + 1, 1).wait()

        @pl.when(c + 2 < nchunks)
        def _():
            idx_copy(c + 2, 0).start()

        issue(c + 1, 1)
        drain(0)
        return carry

    lax.fori_loop(0, nchunks // 2, pair, 0)
    drain(1)


def _gather_rows(idx, x):
    p = idx.shape[0]
    nchunks = p // GATHER_CHUNK
    assert nchunks % 2 == 0 and nchunks * GATHER_CHUNK == p
    return pl.pallas_call(
        functools.partial(_gather_kernel, nchunks=nchunks),
        in_specs=[pl.BlockSpec(memory_space=pl.ANY), pl.BlockSpec(memory_space=pl.ANY)],
        out_specs=pl.BlockSpec(memory_space=pl.ANY),
        out_shape=jax.ShapeDtypeStruct((p, x.shape[1]), x.dtype),
        scratch_shapes=[pltpu.SMEM((GATHER_CHUNK,), jnp.int32), pltpu.SMEM((GATHER_CHUNK,), jnp.int32),
                        pltpu.SemaphoreType.DMA((2,)), pltpu.SemaphoreType.DMA((2,))],
        name="gather_rows",
    )(idx, x)


def _moe_kernel(te_ref, nv_ref, x_ref, gate_ref, wg_ref, wu_ref, wd_ref, o_ref, xb_ref, acc_ref):
    ti = pl.program_id(0)
    f = pl.program_id(1)
    live = ti < nv_ref[0]

    @pl.when(live & (f == 0))
    def _():
        xb_ref[...] = x_ref[...].astype(BF16)
        acc_ref[...] = jnp.zeros(acc_ref.shape, F32)

    @pl.when(live)
    def _():
        xb = xb_ref[...]
        hg = _dot(xb, wg_ref[...])
        hu = _dot(xb, wu_ref[...])
        act = (hg / (1.0 + jnp.exp(-hg)) * hu).astype(BF16)
        acc_ref[...] += _dot(act, wd_ref[...])

    @pl.when(f == pl.num_programs(1) - 1)
    def _():
        o_ref[...] = jnp.where(live, acc_ref[...] * gate_ref[...], 0.0)


def _moe(tile_expert, n_valid, xs, gate_sorted, wg, wu, wd):
    p, d = xs.shape
    tm = MOE_TILE
    fc = MOE_FCHUNK
    nf = wg.shape[2] // fc

    def tile(ti, nv):
        return jnp.minimum(ti, nv[0] - 1)

    grid_spec = pltpu.PrefetchScalarGridSpec(
        num_scalar_prefetch=2,
        grid=(p // tm, nf),
        in_specs=[pl.BlockSpec((tm, d), lambda ti, f, te, nv: (tile(ti, nv), 0)),
                  pl.BlockSpec((tm, 1), lambda ti, f, te, nv: (tile(ti, nv), 0)),
                  pl.BlockSpec((None, d, fc), lambda ti, f, te, nv: (te[tile(ti, nv)], 0, jnp.where(ti < nv[0], f, nf - 1))),
                  pl.BlockSpec((None, d, fc), lambda ti, f, te, nv: (te[tile(ti, nv)], 0, jnp.where(ti < nv[0], f, nf - 1))),
                  pl.BlockSpec((None, fc, d), lambda ti, f, te, nv: (te[tile(ti, nv)], jnp.where(ti < nv[0], f, nf - 1), 0))],
        out_specs=pl.BlockSpec((tm, d), lambda ti, f, te, nv: (ti, 0)),
        scratch_shapes=[pltpu.VMEM((tm, d), BF16), pltpu.VMEM((tm, d), F32)],
    )
    return pl.pallas_call(
        _moe_kernel,
        grid_spec=grid_spec,
        out_shape=jax.ShapeDtypeStruct((p, d), F32),
        compiler_params=_params(("arbitrary", "arbitrary")),
        name="moe_experts",
    )(tile_expert, n_valid, xs, gate_sorted, wg, wu, wd)


def _combine_kernel(pos_hbm, x_ref, g_ref, ys_hbm, o_ref, idx_ref, buf_ref, isem, rsem, *, rt, step0):
    i = pl.program_id(0) + step0
    icopy = pltpu.make_async_copy(pos_hbm.at[pl.ds(pl.multiple_of(i * 2 * rt, 2 * rt), 2 * rt)],
                                  idx_ref, isem)
    icopy.start()
    icopy.wait()

    def row_copy(src, k, r):
        return pltpu.make_async_copy(ys_hbm.at[pl.ds(src, 1)], buf_ref.at[k, pl.ds(r, 1)], rsem)

    def issue(j, carry):
        for k in range(2):
            row_copy(idx_ref[k * rt + j], k, j).start()
        return carry

    def drain(j, carry):
        for k in range(2):
            row_copy(0, k, 0).wait()
        return carry

    lax.fori_loop(0, rt, issue, 0)
    lax.fori_loop(0, rt, drain, 0)
    y = x_ref[...] + buf_ref[0] + buf_ref[1]
    o_ref[...] = _rms(y, g_ref[...])


def _combine(pos_flat, x, g, ys, row0, nrows):
    d = x.shape[1]
    rt = GATHER_CHUNK // 2
    step0 = row0 // rt
    return pl.pallas_call(
        functools.partial(_combine_kernel, rt=rt, step0=step0),
        grid=(nrows // rt,),
        in_specs=[pl.BlockSpec(memory_space=pl.ANY),
                  pl.BlockSpec((rt, d), lambda i: (i + step0, 0)),
                  pl.BlockSpec((1, d), lambda i: (0, 0)),
                  pl.BlockSpec(memory_space=pl.ANY)],
        out_specs=pl.BlockSpec((rt, d), lambda i: (i, 0)),
        out_shape=jax.ShapeDtypeStruct((nrows, d), F32),
        scratch_shapes=[pltpu.SMEM((2 * rt,), jnp.int32), pltpu.VMEM((2, rt, d), F32),
                        pltpu.SemaphoreType.DMA, pltpu.SemaphoreType.DMA],
        compiler_params=_params(("arbitrary",)),
        name="combine",
    )(pos_flat, x, g.reshape(1, d), ys)


def _pad_cols(w, lo, total):
    return jnp.pad(w, ((0, 0), (lo, total - lo - w.shape[1])))


def _rot_half_cols(w):
    half = w.shape[1] // 2
    return jnp.concatenate([-w[:, half:], w[:, :half]], axis=1)


def _a_head_perm():
    half = A_HEADS // 2
    cols = []
    for p in range(half):
        cols += list(range(HEAD_DIM * p, HEAD_DIM * (p + 1)))
        cols += list(range(HEAD_DIM * (p + half), HEAD_DIM * (p + half + 1)))
    return np.asarray(cols, np.int32)


def _rope_tables(t):
    half = C_ROPE // 2
    inv = (ROPE_THETA ** (-np.arange(half) * 2.0 / C_ROPE)).astype(np.float32).astype(np.float64)
    ang = np.arange(t, dtype=np.float64)[:, None] * inv[None, :]
    cos = np.concatenate([np.cos(ang), np.cos(ang)], axis=1)
    sin = np.concatenate([np.sin(ang), np.sin(ang)], axis=1)
    pad = LANES - C_NOPE - C_ROPE
    cos_t = np.concatenate([np.ones((t, C_NOPE)), cos, np.zeros((t, pad))], axis=1)
    sin_t = np.concatenate([np.zeros((t, C_NOPE)), sin, np.zeros((t, pad))], axis=1)
    return jnp.asarray(cos_t, F32), jnp.asarray(sin_t, F32)


def _layer_ab(x, batch, t, norm_attn, w_in, sink, lq1, lk1, lq2, lk2, subln, w_out, norm_ffn,
              w_gate, w_up, w_down, lam_init):
    perm = _a_head_perm()
    qa_w = A_HEADS * HEAD_DIM
    w_in_k = jnp.concatenate([w_in[:, :qa_w][:, perm], w_in[:, qa_w:]], axis=1).astype(BF16)
    scale = HEAD_DIM ** -0.5
    qa, ka, va, qd, kd, vd = _norm_proj(x, norm_attn, w_in_k, AB_WIDTHS,
                                        (scale, 1.0, 1.0, scale, 1.0, 1.0))
    oa = _attn_a(sink, qa, ka, va, batch, t)
    od = _attn_b(lq1, lk1, lq2, lk2, subln, qd, kd, vd, batch, t, lam_init)
    x = _out_proj(x, oa, od, w_out[:qa_w][perm].astype(BF16), w_out[qa_w:].astype(BF16))
    nchunks = w_gate.shape[1] // FFN_CHUNK
    to_chunks = lambda w: w.reshape(w.shape[0], nchunks, FFN_CHUNK).transpose(1, 0, 2).astype(BF16)
    return _ffn(x, norm_ffn, to_chunks(w_gate), to_chunks(w_up), w_down.astype(BF16))


def _cd_weights(w_in, w_uq, w_ukv):
    o = np.cumsum((0,) + CD_WIDTHS)
    w_kr = w_in[:, o[2]:o[3]]
    w_proj = jnp.concatenate([w_in[:, :o[2]], _pad_cols(w_kr, C_NOPE, LANES),
                              _pad_cols(_rot_half_cols(w_kr), C_NOPE, LANES), w_in[:, o[3]:]], axis=1)
    dq = C_NOPE + C_ROPE
    wq1, wq2, wk, wv = [], [], [], []
    for h in range(C_HEADS):
        wq_h = w_uq[:, dq * h:dq * (h + 1)]
        wq1.append(_pad_cols(wq_h, 0, LANES))
        wq2.append(_pad_cols(_rot_half_cols(wq_h[:, C_NOPE:]), C_NOPE, LANES))
        wkv_h = w_ukv[:, (C_NOPE + C_V) * h:(C_NOPE + C_V) * (h + 1)]
        wk.append(_pad_cols(wkv_h[:, :C_NOPE], 0, LANES))
        wv.append(wkv_h[:, C_NOPE:])
    cat = lambda ws: jnp.concatenate(ws, axis=1).astype(BF16)
    return w_proj.astype(BF16), cat(wq1), cat(wq2), cat(wk), cat(wv)


def _route(ids, gates, n):
    e = ids.reshape(-1)
    onehot = (e[:, None] == jnp.arange(N_EXPERTS, dtype=jnp.int32)[None, :]).astype(jnp.int32)
    csum = jnp.cumsum(onehot, axis=0)
    rank = jnp.sum(onehot * (csum - 1), axis=1)
    counts = csum[-1]
    padded = ((counts + MOE_TILE - 1) // MOE_TILE) * MOE_TILE
    gend = jnp.cumsum(padded)
    gstart = gend - padded
    slot = gstart[e] + rank
    p_total = 2 * n + N_EXPERTS * MOE_TILE
    tok = jnp.tile(jnp.arange(n, dtype=jnp.int32), 2)
    src_tok = jnp.zeros((p_total,), jnp.int32).at[slot].set(tok)
    gate_sorted = jnp.zeros((p_total,), F32).at[slot].set(gates.reshape(-1))
    ntiles = p_total // MOE_TILE
    tile_start = jnp.arange(ntiles, dtype=jnp.int32) * MOE_TILE
    tile_expert = jnp.minimum(jnp.sum((tile_start[:, None] >= gend[None, :]).astype(jnp.int32), axis=1),
                              N_EXPERTS - 1).astype(jnp.int32)
    n_valid = (gend[-1] // MOE_TILE).astype(jnp.int32).reshape(1)
    return slot.astype(jnp.int32), src_tok, gate_sorted.reshape(-1, 1), tile_expert, n_valid


def _layer_cd(x, batch, t, norm_attn, w_in, q_norm, kv_norm, w_uq, w_ukv, w_out, norm_ffn, router,
              w_gate, w_up, w_down, final_norm, splits):
    n = x.shape[0]
    w_proj, wq1, wq2, wk, wv = _cd_weights(w_in, w_uq, w_ukv)
    cos_t, sin_t = _rope_tables(t)
    qc, kc, vc, qd, kd, vd = _cd_proj(x, norm_attn, w_proj, q_norm, kv_norm, wq1, wq2, wk, wv,
                                      cos_t, sin_t, t)
    oc = _attn_c(qc, kc, vc, batch, t)
    od = _attn_d(qd, kd, vd, batch, t)
    cw = C_HEADS * C_V
    x = _out_proj(x, oc, od, w_out[:cw].astype(BF16), w_out[cw:].astype(BF16))
    xn, ids, gates = _router(x, norm_ffn, router.T)
    slot, src_tok, gate_sorted, tile_expert, n_valid = _route(ids, gates, n)
    xs = _gather_rows(src_tok, xn)
    ys = _moe(tile_expert, n_valid, xs, gate_sorted, w_gate.astype(BF16), w_up.astype(BF16),
              w_down.astype(BF16))
    rt = GATHER_CHUNK // 2
    pos_flat = slot.reshape(2, n // rt, rt).transpose(1, 0, 2).reshape(-1)
    outs = []
    row0 = 0
    for rows in splits:
        outs.append(_combine(pos_flat, x, final_norm, ys, row0, rows))
        row0 += rows
    return outs


def kernel(x_prompt, x_sample, ab_norm_attn, ab_w_in, ab_sink, ab_lambda_q1, ab_lambda_k1, ab_lambda_q2, ab_lambda_k2, ab_subln, ab_w_out, ab_norm_ffn, ffn_w_gate, ffn_w_up, ffn_w_down, cd_norm_attn, cd_w_in, cd_q_norm, cd_kv_norm, cd_w_uq, cd_w_ukv, cd_w_out, cd_norm_ffn, moe_router, moe_w_gate, moe_w_up, moe_w_down, final_norm):
    bp, t, d = x_prompt.shape
    bs = x_sample.shape[0]
    assert x_sample.shape[1] == t
    batch = bp + bs
    x = jnp.concatenate([x_prompt.reshape(bp * t, d), x_sample.reshape(bs * t, d)], axis=0)
    lam_init = 0.8 - 0.6 * math.exp(-0.3 * 0)
    x = _layer_ab(x, batch, t, ab_norm_attn[0], ab_w_in[0], ab_sink[0], ab_lambda_q1[0],
                  ab_lambda_k1[0], ab_lambda_q2[0], ab_lambda_k2[0], ab_subln[0], ab_w_out[0],
                  ab_norm_ffn[0], ffn_w_gate[0], ffn_w_up[0], ffn_w_down[0], lam_init)
    y_prompt, y_sample = _layer_cd(x, batch, t, cd_norm_attn[0], cd_w_in[0], cd_q_norm[0],
                                   cd_kv_norm[0], cd_w_uq[0], cd_w_ukv[0], cd_w_out[0],
                                   cd_norm_ffn[0], moe_router[0], moe_w_gate[0], moe_w_up[0],
                                   moe_w_down[0], final_norm, (bp * t, bs * t))
    return (y_prompt.reshape(bp, t, d), y_sample.reshape(bs, t, d))
```

```python
import functools
import math

import numpy as np
import jax
import jax.numpy as jnp
from jax import lax
from jax.experimental import pallas as pl
from jax.experimental.pallas import tpu as pltpu

F32 = jnp.float32
BF16 = jnp.bfloat16

D_MODEL = 1024
HEAD_DIM = 64
A_HEADS = 8
A_KV_HEADS = 2
A_WINDOW = 128
A_BLOCK = 128
B_HEADS = 4
B_V_DIM = 2 * HEAD_DIM
C_HEADS = 8
C_Q_RANK = 384
C_KV_RANK = 256
C_NOPE = 64
C_ROPE = 32
C_V = 64
D_HEADS = 8
D_PATTERNS = ((128, 1), (512, 4), (2048, 16))
FFN_DIM = 2816
N_EXPERTS = 8
EXPERT_DIM = 3584
ROPE_THETA = 10000.0
RMS_EPS = 1e-6
NEG_INF = -1e30
LOG2E = math.log2(math.e)
LANES = 128
VMEM_LIMIT = 56 * 1024 * 1024

AB_WIDTHS = (A_HEADS * HEAD_DIM, A_KV_HEADS * HEAD_DIM, A_KV_HEADS * HEAD_DIM,
             B_HEADS * 2 * HEAD_DIM, B_HEADS * 2 * HEAD_DIM, B_HEADS * B_V_DIM)
CD_WIDTHS = (C_Q_RANK, C_KV_RANK, C_ROPE,
             D_HEADS * HEAD_DIM, D_HEADS * HEAD_DIM, D_HEADS * HEAD_DIM)
CD_PROJ_WIDTHS = (C_Q_RANK, C_KV_RANK, LANES, LANES,
                  D_HEADS * HEAD_DIM, D_HEADS * HEAD_DIM, D_HEADS * HEAD_DIM)

ROW_TILE = 512
MOE_TILE = 1024
MOE_FCHUNK = 512
GATHER_CHUNK = 1024
FFN_CHUNK = 256


def _alibi_slopes(n):
    return [float(2.0 ** (-8.0 * (i + 1) / n)) for i in range(n)]


def _params(sem, vmem=VMEM_LIMIT):
    return pltpu.CompilerParams(dimension_semantics=sem, vmem_limit_bytes=vmem)


def _const_spec(a):
    return pl.BlockSpec(a.shape, lambda *_: (0,) * a.ndim, pipeline_mode=pl.Buffered(1))


def _rms(x, g):
    ms = jnp.mean(x * x, axis=-1, keepdims=True)
    return x * lax.rsqrt(ms + RMS_EPS) * g


def _dot(a, b):
    return jnp.dot(a, b, preferred_element_type=F32)


def _dot_nt(a, b):
    return lax.dot_general(a, b, (((1,), (1,)), ((), ())), preferred_element_type=F32)


def _norm_proj_kernel(x_ref, g_ref, w_ref, *o_refs, widths, scales):
    xn = _rms(x_ref[...], g_ref[...]).astype(BF16)
    off = 0
    for o_ref, w, sc in zip(o_refs, widths, scales):
        y = _dot(xn, w_ref[:, off:off + w])
        if sc != 1.0:
            y = y * sc
        o_ref[...] = y.astype(o_ref.dtype)
        off += w


def _norm_proj(x, g, w, widths, scales):
    n, d = x.shape
    tm = min(ROW_TILE, n)
    return pl.pallas_call(
        functools.partial(_norm_proj_kernel, widths=widths, scales=scales),
        grid=(n // tm,),
        in_specs=[pl.BlockSpec((tm, d), lambda i: (i, 0)),
                  pl.BlockSpec((1, d), lambda i: (0, 0)),
                  pl.BlockSpec(w.shape, lambda i: (0, 0))],
        out_specs=[pl.BlockSpec((tm, wd), lambda i: (i, 0)) for wd in widths],
        out_shape=[jax.ShapeDtypeStruct((n, wd), BF16) for wd in widths],
        compiler_params=_params(("parallel",)),
        name="norm_proj",
    )(x, g.reshape(1, d), w)


def _attn_a_kernel(sink_ref, q_ref, kp_ref, kc_ref, kn_ref, vp_ref, vc_ref, vn_ref, o_ref, *, nb):
    i = pl.program_id(1)
    blk = A_BLOCK
    kcat = jnp.concatenate([kp_ref[...], kc_ref[...], kn_ref[...]], axis=0)
    vcat = jnp.concatenate([vp_ref[...], vc_ref[...], vn_ref[...]], axis=0)
    row = lax.broadcasted_iota(jnp.int32, (blk, 3 * blk), 0)
    col = lax.broadcasted_iota(jnp.int32, (blk, 3 * blk), 1)
    dist = jnp.abs(col - blk - row)
    valid = (dist <= A_WINDOW) & ((col >= blk) | (i > 0)) & ((col < 2 * blk) | (i < nb - 1))
    distf = dist.astype(F32)
    lane = lax.broadcasted_iota(jnp.int32, (blk, LANES), 1)
    low = lane < HEAD_DIM
    slopes = _alibi_slopes(A_HEADS)
    half_heads = A_HEADS // 2
    for p in range(half_heads):
        qp = q_ref[:, LANES * p:LANES * (p + 1)]
        res = []
        for half in range(2):
            h = p + half_heads * half
            qm = jnp.where(low if half == 0 else jnp.logical_not(low), qp, jnp.zeros_like(qp))
            s = _dot_nt(qm, kcat)
            s = jnp.where(valid, s - slopes[h] * distf, NEG_INF)
            m = jnp.max(s, axis=-1, keepdims=True)
            e = jnp.exp(s - m)
            l = jnp.sum(e, axis=-1, keepdims=True)
            lse = m + jnp.log(l)
            gate = 1.0 / (1.0 + jnp.exp(sink_ref[h] - lse))
            o = _dot(e.astype(BF16), vcat)
            res.append(o * (gate / l))
        o_ref[:, LANES * p:LANES * (p + 1)] = jnp.where(low, res[0], res[1]).astype(o_ref.dtype)


def _attn_a(sink, qa, ka, va, batch, t):
    nb = t // A_BLOCK
    qw = qa.shape[1]
    kw = ka.shape[1]

    def nbr(delta):
        return lambda b, i: (b * nb + jnp.clip(i + delta, 0, nb - 1), 0)

    kv_specs = [pl.BlockSpec((A_BLOCK, kw), nbr(d)) for d in (-1, 0, 1)]
    return pl.pallas_call(
        functools.partial(_attn_a_kernel, nb=nb),
        grid=(batch, nb),
        in_specs=[pl.BlockSpec(memory_space=pltpu.SMEM),
                  pl.BlockSpec((A_BLOCK, qw), lambda b, i: (b * nb + i, 0))] + kv_specs + kv_specs,
        out_specs=pl.BlockSpec((A_BLOCK, qw), lambda b, i: (b * nb + i, 0)),
        out_shape=jax.ShapeDtypeStruct(qa.shape, BF16),
        compiler_params=_params(("parallel", "parallel")),
        name="attn_a",
    )(sink, qa, ka, ka, ka, va, va, va)


def _softmax_step_t(st, vt, m_ref, l_ref, acc_ref, weight=None):
    m_old = m_ref[...]
    m_new = jnp.maximum(m_old, jnp.max(st, axis=0, keepdims=True))
    alpha = jnp.exp2(m_old - m_new)
    p = jnp.exp2(st - m_new)
    if weight is not None:
        p = p * weight
    l_ref[...] = alpha * l_ref[...] + jnp.sum(p, axis=0, keepdims=True)
    acc_ref[...] = alpha * acc_ref[...] + _dot(vt, p.astype(BF16))
    m_ref[...] = m_new


def _softmax_init(m_ref, l_ref, acc_ref):
    m_ref[...] = jnp.full(m_ref.shape, NEG_INF, F32)
    l_ref[...] = jnp.zeros(l_ref.shape, F32)
    acc_ref[...] = jnp.zeros(acc_ref.shape, F32)


def _fill_vt(v_ref, vt_ref, tk):
    for c in range(vt_ref.shape[0]):
        vt_ref[c] = v_ref[c * tk:(c + 1) * tk, :].astype(F32).T.astype(vt_ref.dtype)


def _attn_scratch(lanes, vdim, t, tk):
    return [pltpu.VMEM((1, lanes), F32), pltpu.VMEM((1, lanes), F32), pltpu.VMEM((LANES, lanes), F32),
            pltpu.VMEM((t // tk, vdim, tk), BF16)]


def _attn_b_kernel(lq1_ref, lk1_ref, lq2_ref, lk2_ref, subln_ref, q_ref, k_ref, v_ref, o_ref,
                   m_ref, l_ref, acc_ref, vt_ref, *, lam_init, t, tq, tk):
    i = pl.program_id(1)

    @pl.when(i == 0)
    def _():
        _fill_vt(v_ref, vt_ref, tk)

    lam = (jnp.exp(jnp.sum(lq1_ref[...] * lk1_ref[...], axis=-1, keepdims=True))
           - jnp.exp(jnp.sum(lq2_ref[...] * lk2_ref[...], axis=-1, keepdims=True)) + lam_init)
    lane = lax.broadcasted_iota(jnp.int32, (tq, LANES), 1)
    low = lane < HEAD_DIM
    rel = (lax.broadcasted_iota(jnp.int32, (tk, tq), 0)
           - lax.broadcasted_iota(jnp.int32, (tk, tq), 1) - i * tq)
    slopes = _alibi_slopes(B_HEADS)
    for h in range(B_HEADS):
        cols = slice(LANES * h, LANES * (h + 1))
        qh = q_ref[:, cols]
        zero = jnp.zeros_like(qh)
        qq = jnp.concatenate([jnp.where(low, qh, zero), jnp.where(low, zero, qh)], axis=0)
        _softmax_init(m_ref, l_ref, acc_ref)

        def body(j, carry, cols=cols, qq=qq, slope=slopes[h] * LOG2E):
            kj = k_ref[pl.ds(pl.multiple_of(j * tk, tk), tk), cols]
            bias = slope * jnp.abs(rel + j * tk).astype(F32)
            st = _dot_nt(kj, qq) - jnp.concatenate([bias, bias], axis=1)
            _softmax_step_t(st, vt_ref[j, cols, :], m_ref, l_ref, acc_ref)
            return carry

        lax.fori_loop(0, t // tk, body, 0)
        o = acc_ref[...] / l_ref[...]
        od = o[:, :tq] - lam * o[:, tq:]
        ms = jnp.mean(od * od, axis=0, keepdims=True)
        y = (od * lax.rsqrt(ms + RMS_EPS)).T * subln_ref[...] * (1.0 - lam_init)
        o_ref[:, cols] = y.astype(o_ref.dtype)


def _attn_b(lq1, lk1, lq2, lk2, subln, qd, kd, vd, batch, t, lam_init):
    tq = min(256, t)
    tk = min(512, t)
    nq = t // tq
    w = qd.shape[1]
    small = lambda a: pl.BlockSpec((1, a.shape[-1]), lambda b, i: (0, 0))
    vecs = [a.reshape(1, -1) for a in (lq1, lk1, lq2, lk2, subln)]
    return pl.pallas_call(
        functools.partial(_attn_b_kernel, lam_init=lam_init, t=t, tq=tq, tk=tk),
        grid=(batch, nq),
        in_specs=[small(a) for a in vecs]
        + [pl.BlockSpec((tq, w), lambda b, i: (b * nq + i, 0)),
           pl.BlockSpec((t, w), lambda b, i: (b, 0)),
           pl.BlockSpec((t, w), lambda b, i: (b, 0))],
        out_specs=pl.BlockSpec((tq, w), lambda b, i: (b * nq + i, 0)),
        out_shape=jax.ShapeDtypeStruct(qd.shape, BF16),
        scratch_shapes=_attn_scratch(2 * tq, w, t, tk),
        compiler_params=_params(("parallel", "arbitrary")),
        name="attn_b",
    )(*vecs, qd, kd, vd)


def _out_proj_kernel(x_ref, a1_ref, a2_ref, w1_ref, w2_ref, o_ref):
    o_ref[...] = x_ref[...] + _dot(a1_ref[...], w1_ref[...]) + _dot(a2_ref[...], w2_ref[...])


def _out_proj(x, a1, a2, w1, w2):
    n, d = x.shape
    tm = min(ROW_TILE, n)
    return pl.pallas_call(
        _out_proj_kernel,
        grid=(n // tm,),
        in_specs=[pl.BlockSpec((tm, d), lambda i: (i, 0)),
                  pl.BlockSpec((tm, a1.shape[1]), lambda i: (i, 0)),
                  pl.BlockSpec((tm, a2.shape[1]), lambda i: (i, 0)),
                  pl.BlockSpec(w1.shape, lambda i: (0, 0)),
                  pl.BlockSpec(w2.shape, lambda i: (0, 0))],
        out_specs=pl.BlockSpec((tm, d), lambda i: (i, 0)),
        out_shape=jax.ShapeDtypeStruct((n, d), F32),
        compiler_params=_params(("parallel",)),
        name="out_proj",
    )(x, a1, a2, w1, w2)


def _ffn_kernel(x_ref, g_ref, wg_ref, wu_ref, wd_ref, o_ref, acc_ref, *, nchunks, fc):
    x = x_ref[...]
    xn = _rms(x, g_ref[...]).astype(BF16)
    acc_ref[...] = x

    def body(c, carry):
        hg = _dot(xn, wg_ref[c])
        hu = _dot(xn, wu_ref[c])
        act = (hg / (1.0 + jnp.exp(-hg)) * hu).astype(BF16)
        acc_ref[...] += _dot(act, wd_ref[pl.ds(pl.multiple_of(c * fc, fc), fc), :])
        return carry

    lax.fori_loop(0, nchunks, body, 0)
    o_ref[...] = acc_ref[...]


def _ffn(x, g, wg3, wu3, wd):
    n, d = x.shape
    nchunks, _, fc = wg3.shape
    tm = min(ROW_TILE, n)
    return pl.pallas_call(
        functools.partial(_ffn_kernel, nchunks=nchunks, fc=fc),
        grid=(n // tm,),
        in_specs=[pl.BlockSpec((tm, d), lambda i: (i, 0)),
                  pl.BlockSpec((1, d), lambda i: (0, 0)),
                  _const_spec(wg3), _const_spec(wu3), _const_spec(wd)],
        out_specs=pl.BlockSpec((tm, d), lambda i: (i, 0)),
        out_shape=jax.ShapeDtypeStruct((n, d), F32),
        scratch_shapes=[pltpu.VMEM((tm, d), F32)],
        compiler_params=_params(("parallel",)),
        name="ffn",
    )(x, g.reshape(1, d), wg3, wu3, wd)


def _cd_proj_kernel(x_ref, g_ref, w_ref, qn_ref, kvn_ref, wq1_ref, wq2_ref, wk_ref, wv_ref,
                    cos_ref, sin_ref, qc_ref, kc_ref, vc_ref, qd_ref, kd_ref, vd_ref):
    h = _rms(x_ref[...], g_ref[...]).astype(BF16)
    offs = np.cumsum((0,) + CD_PROJ_WIDTHS)
    seg = lambda k: _dot(h, w_ref[:, int(offs[k]):int(offs[k + 1])])
    cq, ckv, kr_plain, kr_rot = seg(0), seg(1), seg(2), seg(3)
    qd_ref[...] = (seg(4) * (HEAD_DIM ** -0.5 * LOG2E)).astype(BF16)
    kd_ref[...] = seg(5).astype(BF16)
    vd_ref[...] = seg(6).astype(BF16)
    cqn = _rms(cq, qn_ref[...]).astype(BF16)
    ckvn = _rms(ckv, kvn_ref[...]).astype(BF16)
    cos = cos_ref[...]
    sin = sin_ref[...]
    k_rope = kr_plain * cos + kr_rot * sin
    q1 = _dot(cqn, wq1_ref[...])
    q2 = _dot(cqn, wq2_ref[...])
    kn = _dot(ckvn, wk_ref[...])
    scale = (C_NOPE + C_ROPE) ** -0.5 * LOG2E
    for hh in range(C_HEADS):
        cols = slice(LANES * hh, LANES * (hh + 1))
        qc_ref[:, cols] = ((q1[:, cols] * cos + q2[:, cols] * sin) * scale).astype(BF16)
        kc_ref[:, cols] = (kn[:, cols] + k_rope).astype(BF16)
    vc_ref[...] = _dot(ckvn, wv_ref[...]).astype(BF16)


def _cd_proj(x, g, w, qn, kvn, wq1, wq2, wk, wv, cos_t, sin_t, t):
    n, d = x.shape
    tm = min(ROW_TILE, t)
    nt = t // tm
    full = _const_spec
    consts = [g.reshape(1, d), w, qn.reshape(1, -1), kvn.reshape(1, -1), wq1, wq2, wk, wv]
    out_w = (C_HEADS * LANES, C_HEADS * LANES, C_HEADS * C_V) + CD_PROJ_WIDTHS[4:]
    return pl.pallas_call(
        _cd_proj_kernel,
        grid=(n // tm,),
        in_specs=[pl.BlockSpec((tm, d), lambda i: (i, 0))] + [full(a) for a in consts]
        + [pl.BlockSpec((tm, LANES), lambda i: (i % nt, 0)),
           pl.BlockSpec((tm, LANES), lambda i: (i % nt, 0))],
        out_specs=[pl.BlockSpec((tm, wd), lambda i: (i, 0)) for wd in out_w],
        out_shape=[jax.ShapeDtypeStruct((n, wd), BF16) for wd in out_w],
        compiler_params=_params(("parallel",)),
        name="cd_proj",
    )(x, *consts, cos_t, sin_t)


def _pair_output(o, tq, hd):
    return jnp.concatenate([o[:hd, :tq], o[hd:, tq:]], axis=0).T


def _attn_c_kernel(q_ref, k_ref, v_ref, o_ref, m_ref, l_ref, acc_ref, vt_ref, *, t, tq, tk):
    @pl.when(pl.program_id(2) == 0)
    def _():
        _fill_vt(v_ref, vt_ref, tk)

    q0 = q_ref[:, :LANES]
    q1 = q_ref[:, LANES:]
    _softmax_init(m_ref, l_ref, acc_ref)

    def body(j, carry):
        rows = pl.ds(pl.multiple_of(j * tk, tk), tk)
        st = jnp.concatenate([_dot_nt(k_ref[rows, :LANES], q0), _dot_nt(k_ref[rows, LANES:], q1)],
                             axis=1)
        _softmax_step_t(st, vt_ref[j], m_ref, l_ref, acc_ref)
        return carry

    lax.fori_loop(0, t // tk, body, 0)
    o_ref[...] = _pair_output(acc_ref[...] / l_ref[...], tq, C_V).astype(o_ref.dtype)


def _attn_c(qc, kc, vc, batch, t):
    tq = min(256, t)
    tk = min(512, t)
    nq = t // tq
    pairs = C_HEADS // 2
    return pl.pallas_call(
        functools.partial(_attn_c_kernel, t=t, tq=tq, tk=tk),
        grid=(batch, pairs, nq),
        in_specs=[pl.BlockSpec((tq, 2 * LANES), lambda b, p, i: (b * nq + i, p)),
                  pl.BlockSpec((t, 2 * LANES), lambda b, p, i: (b, p)),
                  pl.BlockSpec((t, LANES), lambda b, p, i: (b, p))],
        out_specs=pl.BlockSpec((tq, LANES), lambda b, p, i: (b * nq + i, p)),
        out_shape=jax.ShapeDtypeStruct(vc.shape, BF16),
        scratch_shapes=_attn_scratch(2 * tq, LANES, t, tk),
        compiler_params=_params(("parallel", "parallel", "arbitrary")),
        name="attn_c",
    )(qc, kc, vc)


def _attn_d_kernel(slope_ref, q_ref, k_ref, v_ref, o_ref, m_ref, l_ref, acc_ref, vt_ref, *, t, tq, tk):
    p = pl.program_id(1)
    i = pl.program_id(2)

    @pl.when(i == 0)
    def _():
        _fill_vt(v_ref, vt_ref, tk)

    reach = max((w // (2 * dil)) * dil for w, dil in D_PATTERNS)
    q = q_ref[...]
    lane = lax.broadcasted_iota(jnp.int32, (tq, LANES), 1)
    low = lane < HEAD_DIM
    zero = jnp.zeros_like(q)
    qq = jnp.concatenate([jnp.where(low, q, zero), jnp.where(low, zero, q)], axis=0)
    s0 = slope_ref[2 * p] * LOG2E
    s1 = slope_ref[2 * p + 1] * LOG2E
    rel = (lax.broadcasted_iota(jnp.int32, (tk, tq), 0)
           - lax.broadcasted_iota(jnp.int32, (tk, tq), 1) - i * tq)
    _softmax_init(m_ref, l_ref, acc_ref)
    c_lo = jnp.maximum(i * tq - reach, 0) // tk
    c_hi = jnp.minimum((i * tq + tq + reach + tk - 1) // tk, t // tk)

    def body(c, carry):
        rows = pl.ds(pl.multiple_of(c * tk, tk), tk)
        d = rel + c * tk
        ad = jnp.abs(d)
        mult = jnp.zeros((tk, tq), F32)
        for w, dil in D_PATTERNS:
            hit = (ad <= (w // (2 * dil)) * dil) & ((d & (dil - 1)) == 0)
            mult = mult + jnp.where(hit, 1.0, 0.0)
        adf = ad.astype(F32)
        bias = jnp.concatenate([s0 * adf, s1 * adf], axis=1)
        mult2 = jnp.concatenate([mult, mult], axis=1)
        st = jnp.where(mult2 > 0.0, _dot_nt(k_ref[rows, :], qq) - bias, NEG_INF)
        _softmax_step_t(st, vt_ref[c], m_ref, l_ref, acc_ref, weight=mult2)
        return carry

    lax.fori_loop(c_lo, c_hi, body, 0)
    o_ref[...] = _pair_output(acc_ref[...] / l_ref[...], tq, HEAD_DIM).astype(o_ref.dtype)


def _attn_d(qd, kd, vd, batch, t):
    tq = min(256, t)
    tk = min(512, t)
    nq = t // tq
    pairs = D_HEADS // 2
    slopes = jnp.asarray(_alibi_slopes(D_HEADS), F32)
    return pl.pallas_call(
        functools.partial(_attn_d_kernel, t=t, tq=tq, tk=tk),
        grid=(batch, pairs, nq),
        in_specs=[pl.BlockSpec(memory_space=pltpu.SMEM),
                  pl.BlockSpec((tq, LANES), lambda b, p, i: (b * nq + i, p)),
                  pl.BlockSpec((t, LANES), lambda b, p, i: (b, p)),
                  pl.BlockSpec((t, LANES), lambda b, p, i: (b, p))],
        out_specs=pl.BlockSpec((tq, LANES), lambda b, p, i: (b * nq + i, p)),
        out_shape=jax.ShapeDtypeStruct(qd.shape, BF16),
        scratch_shapes=_attn_scratch(2 * tq, LANES, t, tk),
        compiler_params=_params(("parallel", "parallel", "arbitrary")),
        name="attn_d",
    )(slopes, qd, kd, vd)


def _router_kernel(x_ref, g_ref, rt_ref, xn_ref, ids_ref, gates_ref):
    xn = _rms(x_ref[...], g_ref[...])
    xn_ref[...] = xn
    x_hi = xn.astype(BF16)
    x_lo = (xn - x_hi.astype(F32)).astype(BF16)
    rt = rt_ref[...]
    r_hi = rt.astype(BF16)
    r_lo = (rt - r_hi.astype(F32)).astype(BF16)
    logits = _dot_nt(r_hi, x_hi) + _dot_nt(r_hi, x_lo) + _dot_nt(r_lo, x_hi)
    eid = lax.broadcasted_iota(jnp.int32, logits.shape, 0)
    v1 = jnp.max(logits, axis=0, keepdims=True)
    i1 = jnp.min(jnp.where(logits == v1, eid, N_EXPERTS), axis=0, keepdims=True)
    rest = jnp.where(eid == i1, -jnp.inf, logits)
    v2 = jnp.max(rest, axis=0, keepdims=True)
    i2 = jnp.min(jnp.where(rest == v2, eid, N_EXPERTS), axis=0, keepdims=True)
    e2 = jnp.exp(v2 - v1)
    g1 = 1.0 / (1.0 + e2)
    ids_ref[...] = jnp.concatenate([i1, i2], axis=0)
    gates_ref[...] = jnp.concatenate([g1, e2 * g1], axis=0)


def _router(x, g, router_t):
    n, d = x.shape
    tm = min(ROW_TILE, n)
    return pl.pallas_call(
        _router_kernel,
        grid=(n // tm,),
        in_specs=[pl.BlockSpec((tm, d), lambda i: (i, 0)),
                  pl.BlockSpec((1, d), lambda i: (0, 0)),
                  pl.BlockSpec(router_t.shape, lambda i: (0, 0))],
        out_specs=[pl.BlockSpec((tm, d), lambda i: (i, 0)),
                   pl.BlockSpec((2, tm), lambda i: (0, i)),
                   pl.BlockSpec((2, tm), lambda i: (0, i))],
        out_shape=[jax.ShapeDtypeStruct((n, d), F32),
                   jax.ShapeDtypeStruct((2, n), jnp.int32),
                   jax.ShapeDtypeStruct((2, n), F32)],
        compiler_params=_params(("parallel",)),
        name="router",
    )(x, g.reshape(1, d), router_t)


def _moe_kernel(te_ref, nv_ref, idx_hbm, x_hbm, gate_ref, wg_ref, wu_ref, wd_ref, o_ref,
                idx_a, idx_b, xbuf_ref, xb_ref, acc_ref, isem, rsem):
    ti = pl.program_id(0)
    f = pl.program_id(1)
    nv = nv_ref[0]
    live = ti < nv
    tm = xb_ref.shape[0]
    idx_slots = (idx_a, idx_b)

    def idx_copy(tile, slot):
        return pltpu.make_async_copy(idx_hbm.at[pl.ds(pl.multiple_of(tile * tm, tm), tm)],
                                     idx_slots[slot], isem.at[slot])

    def row_copy(tok, r, slot):
        return pltpu.make_async_copy(x_hbm.at[pl.ds(tok, 1)], xbuf_ref.at[slot, pl.ds(r, 1)],
                                     rsem.at[slot])

    def issue_rows(slot):
        def row(r, carry):
            row_copy(idx_slots[slot][r], r, slot).start()
            return carry
        lax.fori_loop(0, tm, row, 0, unroll=8)

    def wait_rows(slot):
        def row(r, carry):
            row_copy(0, 0, slot).wait()
            return carry
        lax.fori_loop(0, tm, row, 0, unroll=8)

    def start_tile(slot):
        wait_rows(slot)

        @pl.when(ti + 1 < nv)
        def _():
            idx_copy(ti + 1, 1 - slot).wait()
            issue_rows(1 - slot)

        @pl.when(ti + 2 < nv)
        def _():
            idx_copy(ti + 2, slot).start()

        xb_ref[...] = xbuf_ref[slot].astype(BF16)
        acc_ref[...] = jnp.zeros(acc_ref.shape, F32)

    @pl.when((ti == 0) & (f == 0))
    def _():
        idx_copy(0, 0).start()
        idx_copy(0, 0).wait()
        issue_rows(0)

        @pl.when(1 < nv)
        def _():
            idx_copy(1, 1).start()

    for slot in range(2):
        @pl.when(live & (f == 0) & (ti % 2 == slot))
        def _(slot=slot):
            start_tile(slot)

    @pl.when(live)
    def _():
        xb = xb_ref[...]
        hg = _dot(xb, wg_ref[...])
        hu = _dot(xb, wu_ref[...])
        act = (hg / (1.0 + jnp.exp(-hg)) * hu).astype(BF16)
        acc_ref[...] += _dot(act, wd_ref[...])

    @pl.when(f == pl.num_programs(1) - 1)
    def _():
        o_ref[...] = jnp.where(live, acc_ref[...] * gate_ref[...], 0.0)


def _moe(tile_expert, n_valid, src_tok, xn, gate_sorted, wg, wu, wd):
    p = src_tok.shape[0]
    d = xn.shape[1]
    tm = MOE_TILE
    fc = MOE_FCHUNK
    nf = wg.shape[2] // fc

    def tile(ti, nv):
        return jnp.minimum(ti, nv[0] - 1)

    grid_spec = pltpu.PrefetchScalarGridSpec(
        num_scalar_prefetch=2,
        grid=(p // tm, nf),
        in_specs=[pl.BlockSpec(memory_space=pl.ANY), pl.BlockSpec(memory_space=pl.ANY),
                  pl.BlockSpec((tm, 1), lambda ti, f, te, nv: (tile(ti, nv), 0)),
                  pl.BlockSpec((None, d, fc), lambda ti, f, te, nv: (te[tile(ti, nv)], 0, jnp.where(ti < nv[0], f, nf - 1))),
                  pl.BlockSpec((None, d, fc), lambda ti, f, te, nv: (te[tile(ti, nv)], 0, jnp.where(ti < nv[0], f, nf - 1))),
                  pl.BlockSpec((None, fc, d), lambda ti, f, te, nv: (te[tile(ti, nv)], jnp.where(ti < nv[0], f, nf - 1), 0))],
        out_specs=pl.BlockSpec((tm, d), lambda ti, f, te, nv: (ti, 0)),
        scratch_shapes=[pltpu.SMEM((tm,), jnp.int32), pltpu.SMEM((tm,), jnp.int32),
                        pltpu.VMEM((2, tm, d), F32), pltpu.VMEM((tm, d), BF16), pltpu.VMEM((tm, d), F32),
                        pltpu.SemaphoreType.DMA((2,)), pltpu.SemaphoreType.DMA((2,))],
    )
    return pl.pallas_call(
        _moe_kernel,
        grid_spec=grid_spec,
        out_shape=jax.ShapeDtypeStruct((p, d), F32),
        compiler_params=_params(("arbitrary", "arbitrary")),
        name="moe_experts",
    )(tile_expert, n_valid, src_tok, xn, gate_sorted, wg, wu, wd)


def _combine_kernel(pos_hbm, x_ref, g_ref, ys_hbm, o_ref, idx_ref, buf_ref, isem, rsem, *, rt, step0):
    i = pl.program_id(0) + step0
    icopy = pltpu.make_async_copy(pos_hbm.at[pl.ds(pl.multiple_of(i * 2 * rt, 2 * rt), 2 * rt)],
                                  idx_ref, isem)
    icopy.start()
    icopy.wait()

    def row_copy(src, k, r):
        return pltpu.make_async_copy(ys_hbm.at[pl.ds(src, 1)], buf_ref.at[k, pl.ds(r, 1)], rsem)

    def issue(j, carry):
        for k in range(2):
            row_copy(idx_ref[k * rt + j], k, j).start()
        return carry

    def drain(j, carry):
        for k in range(2):
            row_copy(0, k, 0).wait()
        return carry

    lax.fori_loop(0, rt, issue, 0)
    lax.fori_loop(0, rt, drain, 0)
    y = x_ref[...] + buf_ref[0] + buf_ref[1]
    o_ref[...] = _rms(y, g_ref[...])


def _combine(pos_flat, x, g, ys, row0, nrows):
    d = x.shape[1]
    rt = GATHER_CHUNK // 2
    step0 = row0 // rt
    return pl.pallas_call(
        functools.partial(_combine_kernel, rt=rt, step0=step0),
        grid=(nrows // rt,),
        in_specs=[pl.BlockSpec(memory_space=pl.ANY),
                  pl.BlockSpec((rt, d), lambda i: (i + step0, 0)),
                  pl.BlockSpec((1, d), lambda i: (0, 0)),
                  pl.BlockSpec(memory_space=pl.ANY)],
        out_specs=pl.BlockSpec((rt, d), lambda i: (i, 0)),
        out_shape=jax.ShapeDtypeStruct((nrows, d), F32),
        scratch_shapes=[pltpu.SMEM((2 * rt,), jnp.int32), pltpu.VMEM((2, rt, d), F32),
                        pltpu.SemaphoreType.DMA, pltpu.SemaphoreType.DMA],
        compiler_params=_params(("arbitrary",)),
        name="combine",
    )(pos_flat, x, g.reshape(1, d), ys)


def _pad_cols(w, lo, total):
    return jnp.pad(w, ((0, 0), (lo, total - lo - w.shape[1])))


def _rot_half_cols(w):
    half = w.shape[1] // 2
    return jnp.concatenate([-w[:, half:], w[:, :half]], axis=1)


def _a_head_perm():
    half = A_HEADS // 2
    cols = []
    for p in range(half):
        cols += list(range(HEAD_DIM * p, HEAD_DIM * (p + 1)))
        cols += list(range(HEAD_DIM * (p + half), HEAD_DIM * (p + half + 1)))
    return np.asarray(cols, np.int32)


def _rope_tables(t):
    half = C_ROPE // 2
    inv = (ROPE_THETA ** (-np.arange(half) * 2.0 / C_ROPE)).astype(np.float32).astype(np.float64)
    ang = np.arange(t, dtype=np.float64)[:, None] * inv[None, :]
    cos = np.concatenate([np.cos(ang), np.cos(ang)], axis=1)
    sin = np.concatenate([np.sin(ang), np.sin(ang)], axis=1)
    pad = LANES - C_NOPE - C_ROPE
    cos_t = np.concatenate([np.ones((t, C_NOPE)), cos, np.zeros((t, pad))], axis=1)
    sin_t = np.concatenate([np.zeros((t, C_NOPE)), sin, np.zeros((t, pad))], axis=1)
    return jnp.asarray(cos_t, F32), jnp.asarray(sin_t, F32)


def _layer_ab(x, batch, t, norm_attn, w_in, sink, lq1, lk1, lq2, lk2, subln, w_out, norm_ffn,
              w_gate, w_up, w_down, lam_init):
    perm = _a_head_perm()
    qa_w = A_HEADS * HEAD_DIM
    w_in_k = jnp.concatenate([w_in[:, :qa_w][:, perm], w_in[:, qa_w:]], axis=1).astype(BF16)
    scale = HEAD_DIM ** -0.5
    qa, ka, va, qd, kd, vd = _norm_proj(x, norm_attn, w_in_k, AB_WIDTHS,
                                        (scale, 1.0, 1.0, scale * LOG2E, 1.0, 1.0))
    oa = _attn_a(sink, qa, ka, va, batch, t)
    od = _attn_b(lq1, lk1, lq2, lk2, subln, qd, kd, vd, batch, t, lam_init)
    x = _out_proj(x, oa, od, w_out[:qa_w][perm].astype(BF16), w_out[qa_w:].astype(BF16))
    nchunks = w_gate.shape[1] // FFN_CHUNK
    to_chunks = lambda w: w.reshape(w.shape[0], nchunks, FFN_CHUNK).transpose(1, 0, 2).astype(BF16)
    return _ffn(x, norm_ffn, to_chunks(w_gate), to_chunks(w_up), w_down.astype(BF16))


def _cd_weights(w_in, w_uq, w_ukv):
    o = np.cumsum((0,) + CD_WIDTHS)
    w_kr = w_in[:, o[2]:o[3]]
    w_proj = jnp.concatenate([w_in[:, :o[2]], _pad_cols(w_kr, C_NOPE, LANES),
                              _pad_cols(_rot_half_cols(w_kr), C_NOPE, LANES), w_in[:, o[3]:]], axis=1)
    dq = C_NOPE + C_ROPE
    wq1, wq2, wk, wv = [], [], [], []
    for h in range(C_HEADS):
        wq_h = w_uq[:, dq * h:dq * (h + 1)]
        wq1.append(_pad_cols(wq_h, 0, LANES))
        wq2.append(_pad_cols(_rot_half_cols(wq_h[:, C_NOPE:]), C_NOPE, LANES))
        wkv_h = w_ukv[:, (C_NOPE + C_V) * h:(C_NOPE + C_V) * (h + 1)]
        wk.append(_pad_cols(wkv_h[:, :C_NOPE], 0, LANES))
        wv.append(wkv_h[:, C_NOPE:])
    cat = lambda ws: jnp.concatenate(ws, axis=1).astype(BF16)
    return w_proj.astype(BF16), cat(wq1), cat(wq2), cat(wk), cat(wv)


def _route(ids, gates, n):
    e = ids.reshape(-1)
    onehot = (e[:, None] == jnp.arange(N_EXPERTS, dtype=jnp.int32)[None, :]).astype(jnp.int32)
    csum = jnp.cumsum(onehot, axis=0)
    rank = jnp.sum(onehot * (csum - 1), axis=1)
    counts = csum[-1]
    padded = ((counts + MOE_TILE - 1) // MOE_TILE) * MOE_TILE
    gend = jnp.cumsum(padded)
    gstart = gend - padded
    slot = gstart[e] + rank
    p_total = 2 * n + N_EXPERTS * MOE_TILE
    tok = jnp.tile(jnp.arange(n, dtype=jnp.int32), 2)
    src_tok = jnp.zeros((p_total,), jnp.int32).at[slot].set(tok)
    gate_sorted = jnp.zeros((p_total,), F32).at[slot].set(gates.reshape(-1))
    ntiles = p_total // MOE_TILE
    tile_start = jnp.arange(ntiles, dtype=jnp.int32) * MOE_TILE
    tile_expert = jnp.minimum(jnp.sum((tile_start[:, None] >= gend[None, :]).astype(jnp.int32), axis=1),
                              N_EXPERTS - 1).astype(jnp.int32)
    n_valid = (gend[-1] // MOE_TILE).astype(jnp.int32).reshape(1)
    return slot.astype(jnp.int32), src_tok, gate_sorted.reshape(-1, 1), tile_expert, n_valid


def _layer_cd(x, batch, t, norm_attn, w_in, q_norm, kv_norm, w_uq, w_ukv, w_out, norm_ffn, router,
              w_gate, w_up, w_down, final_norm, splits):
    n = x.shape[0]
    w_proj, wq1, wq2, wk, wv = _cd_weights(w_in, w_uq, w_ukv)
    cos_t, sin_t = _rope_tables(t)
    qc, kc, vc, qd, kd, vd = _cd_proj(x, norm_attn, w_proj, q_norm, kv_norm, wq1, wq2, wk, wv,
                                      cos_t, sin_t, t)
    oc = _attn_c(qc, kc, vc, batch, t)
    od = _attn_d(qd, kd, vd, batch, t)
    cw = C_HEADS * C_V
    x = _out_proj(x, oc, od, w_out[:cw].astype(BF16), w_out[cw:].astype(BF16))
    xn, ids, gates = _router(x, norm_ffn, router.T)
    slot, src_tok, gate_sorted, tile_expert, n_valid = _route(ids, gates, n)
    ys = _moe(tile_expert, n_valid, src_tok, xn, gate_sorted, w_gate.astype(BF16), w_up.astype(BF16),
              w_down.astype(BF16))
    rt = GATHER_CHUNK // 2
    pos_flat = slot.reshape(2, n // rt, rt).transpose(1, 0, 2).reshape(-1)
    outs = []
    row0 = 0
    for rows in splits:
        outs.append(_combine(pos_flat, x, final_norm, ys, row0, rows))
        row0 += rows
    return outs


def kernel(x_prompt, x_sample, ab_norm_attn, ab_w_in, ab_sink, ab_lambda_q1, ab_lambda_k1, ab_lambda_q2, ab_lambda_k2, ab_subln, ab_w_out, ab_norm_ffn, ffn_w_gate, ffn_w_up, ffn_w_down, cd_norm_attn, cd_w_in, cd_q_norm, cd_kv_norm, cd_w_uq, cd_w_ukv, cd_w_out, cd_norm_ffn, moe_router, moe_w_gate, moe_w_up, moe_w_down, final_norm):
    bp, t, d = x_prompt.shape
    bs = x_sample.shape[0]
    assert x_sample.shape[1] == t
    batch = bp + bs
    x = jnp.concatenate([x_prompt.reshape(bp * t, d), x_sample.reshape(bs * t, d)], axis=0)
    lam_init = 0.8 - 0.6 * math.exp(-0.3 * 0)
    x = _layer_ab(x, batch, t, ab_norm_attn[0], ab_w_in[0], ab_sink[0], ab_lambda_q1[0],
                  ab_lambda_k1[0], ab_lambda_q2[0], ab_lambda_k2[0], ab_subln[0], ab_w_out[0],
                  ab_norm_ffn[0], ffn_w_gate[0], ffn_w_up[0], ffn_w_down[0], lam_init)
    y_prompt, y_sample = _layer_cd(x, batch, t, cd_norm_attn[0], cd_w_in[0], cd_q_norm[0],
                                   cd_kv_norm[0], cd_w_uq[0], cd_w_ukv[0], cd_w_out[0],
                                   cd_norm_ffn[0], moe_router[0], moe_w_gate[0], moe_w_up[0],
                                   moe_w_down[0], final_norm, (bp * t, bs * t))
    return (y_prompt.reshape(bp, t, d), y_sample.reshape(bs, t, d))
```

```python
import functools
import math

import numpy as np
import jax
import jax.numpy as jnp
from jax import lax
from jax.experimental import pallas as pl
from jax.experimental.pallas import tpu as pltpu

F32 = jnp.float32
BF16 = jnp.bfloat16

D_MODEL = 1024
HEAD_DIM = 64
A_HEADS = 8
A_KV_HEADS = 2
A_WINDOW = 128
A_BLOCK = 128
B_HEADS = 4
B_V_DIM = 2 * HEAD_DIM
C_HEADS = 8
C_Q_RANK = 384
C_KV_RANK = 256
C_NOPE = 64
C_ROPE = 32
C_V = 64
D_HEADS = 8
D_PATTERNS = ((128, 1), (512, 4), (2048, 16))
FFN_DIM = 2816
N_EXPERTS = 8
EXPERT_DIM = 3584
ROPE_THETA = 10000.0
RMS_EPS = 1e-6
NEG_INF = -1e30
LOG2E = math.log2(math.e)
LANES = 128
VMEM_LIMIT = 56 * 1024 * 1024

AB_WIDTHS = (A_HEADS * HEAD_DIM, A_KV_HEADS * HEAD_DIM, A_KV_HEADS * HEAD_DIM,
             B_HEADS * 2 * HEAD_DIM, B_HEADS * 2 * HEAD_DIM, B_HEADS * B_V_DIM)
CD_WIDTHS = (C_Q_RANK, C_KV_RANK, C_ROPE,
             D_HEADS * HEAD_DIM, D_HEADS * HEAD_DIM, D_HEADS * HEAD_DIM)
CD_PROJ_WIDTHS = (C_Q_RANK, C_KV_RANK, LANES, LANES,
                  D_HEADS * HEAD_DIM, D_HEADS * HEAD_DIM, D_HEADS * HEAD_DIM)

ROW_TILE = 512
MOE_TILE = 1024
MOE_FCHUNK = 512
GATHER_CHUNK = 1024
FFN_CHUNK = 256


def _alibi_slopes(n):
    return [float(2.0 ** (-8.0 * (i + 1) / n)) for i in range(n)]


def _params(sem, vmem=VMEM_LIMIT):
    return pltpu.CompilerParams(dimension_semantics=sem, vmem_limit_bytes=vmem)


def _const_spec(a):
    return pl.BlockSpec(a.shape, lambda *_: (0,) * a.ndim, pipeline_mode=pl.Buffered(1))


def _rms(x, g):
    ms = jnp.mean(x * x, axis=-1, keepdims=True)
    return x * lax.rsqrt(ms + RMS_EPS) * g


def _dot(a, b):
    return jnp.dot(a, b, preferred_element_type=F32)


def _dot_nt(a, b):
    return lax.dot_general(a, b, (((1,), (1,)), ((), ())), preferred_element_type=F32)


def _norm_proj_kernel(x_ref, g_ref, w_ref, *o_refs, widths, scales):
    xn = _rms(x_ref[...], g_ref[...]).astype(BF16)
    off = 0
    for o_ref, w, sc in zip(o_refs, widths, scales):
        y = _dot(xn, w_ref[:, off:off + w])
        if sc != 1.0:
            y = y * sc
        o_ref[...] = y.astype(o_ref.dtype)
        off += w


def _norm_proj(x, g, w, widths, scales):
    n, d = x.shape
    tm = min(ROW_TILE, n)
    return pl.pallas_call(
        functools.partial(_norm_proj_kernel, widths=widths, scales=scales),
        grid=(n // tm,),
        in_specs=[pl.BlockSpec((tm, d), lambda i: (i, 0)),
                  pl.BlockSpec((1, d), lambda i: (0, 0)),
                  pl.BlockSpec(w.shape, lambda i: (0, 0))],
        out_specs=[pl.BlockSpec((tm, wd), lambda i: (i, 0)) for wd in widths],
        out_shape=[jax.ShapeDtypeStruct((n, wd), BF16) for wd in widths],
        compiler_params=_params(("parallel",)),
        name="norm_proj",
    )(x, g.reshape(1, d), w)


def _attn_a_kernel(sink_ref, q_ref, kp_ref, kc_ref, kn_ref, vp_ref, vc_ref, vn_ref, o_ref, *, nb):
    i = pl.program_id(1)
    blk = A_BLOCK
    kcat = jnp.concatenate([kp_ref[...], kc_ref[...], kn_ref[...]], axis=0)
    vcat = jnp.concatenate([vp_ref[...], vc_ref[...], vn_ref[...]], axis=0)
    row = lax.broadcasted_iota(jnp.int32, (blk, 3 * blk), 0)
    col = lax.broadcasted_iota(jnp.int32, (blk, 3 * blk), 1)
    dist = jnp.abs(col - blk - row)
    valid = (dist <= A_WINDOW) & ((col >= blk) | (i > 0)) & ((col < 2 * blk) | (i < nb - 1))
    distf = dist.astype(F32)
    lane = lax.broadcasted_iota(jnp.int32, (blk, LANES), 1)
    low = lane < HEAD_DIM
    slopes = _alibi_slopes(A_HEADS)
    half_heads = A_HEADS // 2
    for p in range(half_heads):
        qp = q_ref[:, LANES * p:LANES * (p + 1)]
        res = []
        for half in range(2):
            h = p + half_heads * half
            qm = jnp.where(low if half == 0 else jnp.logical_not(low), qp, jnp.zeros_like(qp))
            s = _dot_nt(qm, kcat)
            s = jnp.where(valid, s - slopes[h] * distf, NEG_INF)
            m = jnp.max(s, axis=-1, keepdims=True)
            e = jnp.exp(s - m)
            l = jnp.sum(e, axis=-1, keepdims=True)
            lse = m + jnp.log(l)
            gate = 1.0 / (1.0 + jnp.exp(sink_ref[h] - lse))
            o = _dot(e.astype(BF16), vcat)
            res.append(o * (gate / l))
        o_ref[:, LANES * p:LANES * (p + 1)] = jnp.where(low, res[0], res[1]).astype(o_ref.dtype)


def _attn_a(sink, qa, ka, va, batch, t):
    nb = t // A_BLOCK
    qw = qa.shape[1]
    kw = ka.shape[1]

    def nbr(delta):
        return lambda b, i: (b * nb + jnp.clip(i + delta, 0, nb - 1), 0)

    kv_specs = [pl.BlockSpec((A_BLOCK, kw), nbr(d)) for d in (-1, 0, 1)]
    return pl.pallas_call(
        functools.partial(_attn_a_kernel, nb=nb),
        grid=(batch, nb),
        in_specs=[pl.BlockSpec(memory_space=pltpu.SMEM),
                  pl.BlockSpec((A_BLOCK, qw), lambda b, i: (b * nb + i, 0))] + kv_specs + kv_specs,
        out_specs=pl.BlockSpec((A_BLOCK, qw), lambda b, i: (b * nb + i, 0)),
        out_shape=jax.ShapeDtypeStruct(qa.shape, BF16),
        compiler_params=_params(("parallel", "parallel")),
        name="attn_a",
    )(sink, qa, ka, ka, ka, va, va, va)


def _softmax_step_t(st, vt, m_ref, l_ref, acc_ref, offset=None):
    m_old = m_ref[...]
    smax = jnp.max(st, axis=0, keepdims=True)
    if offset is None:
        m_new = jnp.maximum(m_old, smax)
        shift = m_new
    else:
        m_new = jnp.maximum(m_old, smax - offset)
        shift = m_new + offset
    alpha = jnp.exp2(m_old - m_new)
    p = jnp.exp2(st - shift)
    l_ref[...] = alpha * l_ref[...] + jnp.sum(p, axis=0, keepdims=True)
    acc_ref[...] = alpha * acc_ref[...] + _dot(vt, p.astype(BF16))
    m_ref[...] = m_new


MASKED = 1e30
M_INIT = -1e29


def _softmax_init(m_ref, l_ref, acc_ref):
    m_ref[...] = jnp.full(m_ref.shape, M_INIT, F32)
    l_ref[...] = jnp.zeros(l_ref.shape, F32)
    acc_ref[...] = jnp.zeros(acc_ref.shape, F32)


def _fill_vt(v_ref, vt_ref, tk):
    for c in range(vt_ref.shape[0]):
        vt_ref[c] = v_ref[c * tk:(c + 1) * tk, :].astype(F32).T.astype(vt_ref.dtype)


def _attn_scratch(lanes, vdim, t, tk):
    return [pltpu.VMEM((1, lanes), F32), pltpu.VMEM((1, lanes), F32), pltpu.VMEM((LANES, lanes), F32),
            pltpu.VMEM((t // tk, vdim, tk), BF16),
            pltpu.VMEM((tk, lanes), F32), pltpu.VMEM((tk, lanes), F32)]


def _attn_b_kernel(lq1_ref, lk1_ref, lq2_ref, lk2_ref, subln_ref, q_ref, k_ref, v_ref, o_ref,
                   m_ref, l_ref, acc_ref, vt_ref, sa_ref, sb_ref, bias_ref, *, lam_init, t, tq, tk):
    i = pl.program_id(1)
    r = tk // tq
    nk = t // tk
    slopes = [s * LOG2E for s in _alibi_slopes(B_HEADS)]

    @pl.when(i == 0)
    def _():
        _fill_vt(v_ref, vt_ref, tk)
        amb = (lax.broadcasted_iota(jnp.int32, (tk, tq), 0)
               - lax.broadcasted_iota(jnp.int32, (tk, tq), 1))
        for h in range(B_HEADS):
            lin = slopes[h] * amb.astype(F32)
            bias_ref[h, 0] = lin
            bias_ref[h, 1] = -lin
            for s in range(r):
                bias_ref[h, 2 + s] = slopes[h] * jnp.abs(amb - s * tq).astype(F32)

    lam = (jnp.exp(jnp.sum(lq1_ref[...] * lk1_ref[...], axis=-1, keepdims=True))
           - jnp.exp(jnp.sum(lq2_ref[...] * lk2_ref[...], axis=-1, keepdims=True)) + lam_init)
    lane = lax.broadcasted_iota(jnp.int32, (tq, LANES), 1)
    low = lane < HEAD_DIM
    jd = i // r
    for h in range(B_HEADS):
        cols = slice(LANES * h, LANES * (h + 1))
        qh = q_ref[:, cols]
        zero = jnp.zeros_like(qh)
        qq = jnp.concatenate([jnp.where(low, qh, zero), jnp.where(low, zero, qh)], axis=0)
        _softmax_init(m_ref, l_ref, acc_ref)

        def scores(j, h=h, cols=cols, qq=qq):
            kj = k_ref[pl.ds(pl.multiple_of(j * tk, tk), tk), cols]
            sel = jnp.where(j > jd, 0, jnp.where(j < jd, 1, 2 + i % r))
            bias = bias_ref[h, sel]
            return _dot_nt(kj, qq) - jnp.concatenate([bias, bias], axis=1)

        def offset(j, slope=slopes[h]):
            sign = jnp.where(j > jd, 1.0, jnp.where(j < jd, -1.0, 0.0))
            return sign * slope * (j * tk - i * tq).astype(F32)

        sa_ref[...] = scores(0)

        def body(jj, carry, h=h, cols=cols, scores=scores, offset=offset):
            j = 2 * jj
            sb_ref[...] = scores(j + 1)
            _softmax_step_t(sa_ref[...], vt_ref[j, cols, :], m_ref, l_ref, acc_ref, offset(j))
            sa_ref[...] = scores(jnp.minimum(j + 2, nk - 1))
            _softmax_step_t(sb_ref[...], vt_ref[j + 1, cols, :], m_ref, l_ref, acc_ref, offset(j + 1))
            return carry

        lax.fori_loop(0, nk // 2, body, 0)
        o = acc_ref[...] / l_ref[...]
        od = o[:, :tq] - lam * o[:, tq:]
        ms = jnp.mean(od * od, axis=0, keepdims=True)
        y = (od * lax.rsqrt(ms + RMS_EPS)).T * subln_ref[...] * (1.0 - lam_init)
        o_ref[:, cols] = y.astype(o_ref.dtype)


def _attn_b(lq1, lk1, lq2, lk2, subln, qd, kd, vd, batch, t, lam_init):
    tq = min(256, t)
    tk = min(512, t)
    nq = t // tq
    w = qd.shape[1]
    small = lambda a: pl.BlockSpec((1, a.shape[-1]), lambda b, i: (0, 0))
    vecs = [a.reshape(1, -1) for a in (lq1, lk1, lq2, lk2, subln)]
    return pl.pallas_call(
        functools.partial(_attn_b_kernel, lam_init=lam_init, t=t, tq=tq, tk=tk),
        grid=(batch, nq),
        in_specs=[small(a) for a in vecs]
        + [pl.BlockSpec((tq, w), lambda b, i: (b * nq + i, 0)),
           pl.BlockSpec((t, w), lambda b, i: (b, 0)),
           pl.BlockSpec((t, w), lambda b, i: (b, 0))],
        out_specs=pl.BlockSpec((tq, w), lambda b, i: (b * nq + i, 0)),
        out_shape=jax.ShapeDtypeStruct(qd.shape, BF16),
        scratch_shapes=_attn_scratch(2 * tq, w, t, tk)
        + [pltpu.VMEM((B_HEADS, 2 + tk // tq, tk, tq), F32)],
        compiler_params=_params(("parallel", "arbitrary")),
        name="attn_b",
    )(*vecs, qd, kd, vd)


def _out_proj_kernel(x_ref, a1_ref, a2_ref, w1_ref, w2_ref, o_ref):
    o_ref[...] = x_ref[...] + _dot(a1_ref[...], w1_ref[...]) + _dot(a2_ref[...], w2_ref[...])


def _out_proj(x, a1, a2, w1, w2):
    n, d = x.shape
    tm = min(ROW_TILE, n)
    return pl.pallas_call(
        _out_proj_kernel,
        grid=(n // tm,),
        in_specs=[pl.BlockSpec((tm, d), lambda i: (i, 0)),
                  pl.BlockSpec((tm, a1.shape[1]), lambda i: (i, 0)),
                  pl.BlockSpec((tm, a2.shape[1]), lambda i: (i, 0)),
                  pl.BlockSpec(w1.shape, lambda i: (0, 0)),
                  pl.BlockSpec(w2.shape, lambda i: (0, 0))],
        out_specs=pl.BlockSpec((tm, d), lambda i: (i, 0)),
        out_shape=jax.ShapeDtypeStruct((n, d), F32),
        compiler_params=_params(("parallel",)),
        name="out_proj",
    )(x, a1, a2, w1, w2)


def _ffn_kernel(x_ref, g_ref, wg_ref, wu_ref, wd_ref, o_ref, acc_ref, *, nchunks, fc):
    x = x_ref[...]
    xn = _rms(x, g_ref[...]).astype(BF16)
    acc_ref[...] = x

    def body(c, carry):
        hg = _dot(xn, wg_ref[c])
        hu = _dot(xn, wu_ref[c])
        act = (hg / (1.0 + jnp.exp(-hg)) * hu).astype(BF16)
        acc_ref[...] += _dot(act, wd_ref[pl.ds(pl.multiple_of(c * fc, fc), fc), :])
        return carry

    lax.fori_loop(0, nchunks, body, 0)
    o_ref[...] = acc_ref[...]


def _ffn(x, g, wg3, wu3, wd):
    n, d = x.shape
    nchunks, _, fc = wg3.shape
    tm = min(ROW_TILE, n)
    return pl.pallas_call(
        functools.partial(_ffn_kernel, nchunks=nchunks, fc=fc),
        grid=(n // tm,),
        in_specs=[pl.BlockSpec((tm, d), lambda i: (i, 0)),
                  pl.BlockSpec((1, d), lambda i: (0, 0)),
                  _const_spec(wg3), _const_spec(wu3), _const_spec(wd)],
        out_specs=pl.BlockSpec((tm, d), lambda i: (i, 0)),
        out_shape=jax.ShapeDtypeStruct((n, d), F32),
        scratch_shapes=[pltpu.VMEM((tm, d), F32)],
        compiler_params=_params(("parallel",)),
        name="ffn",
    )(x, g.reshape(1, d), wg3, wu3, wd)


def _cd_proj_kernel(x_ref, g_ref, w_ref, qn_ref, kvn_ref, wq1_ref, wq2_ref, wk_ref, wv_ref,
                    cos_ref, sin_ref, qc_ref, kc_ref, vc_ref, qd_ref, kd_ref, vd_ref):
    h = _rms(x_ref[...], g_ref[...]).astype(BF16)
    offs = np.cumsum((0,) + CD_PROJ_WIDTHS)
    seg = lambda k: _dot(h, w_ref[:, int(offs[k]):int(offs[k + 1])])
    cq, ckv, kr_plain, kr_rot = seg(0), seg(1), seg(2), seg(3)
    qd_ref[...] = (seg(4) * (HEAD_DIM ** -0.5 * LOG2E)).astype(BF16)
    kd_ref[...] = seg(5).astype(BF16)
    vd_ref[...] = seg(6).astype(BF16)
    cqn = _rms(cq, qn_ref[...]).astype(BF16)
    ckvn = _rms(ckv, kvn_ref[...]).astype(BF16)
    cos = cos_ref[...]
    sin = sin_ref[...]
    k_rope = kr_plain * cos + kr_rot * sin
    q1 = _dot(cqn, wq1_ref[...])
    q2 = _dot(cqn, wq2_ref[...])
    kn = _dot(ckvn, wk_ref[...])
    scale = (C_NOPE + C_ROPE) ** -0.5 * LOG2E
    for hh in range(C_HEADS):
        cols = slice(LANES * hh, LANES * (hh + 1))
        qc_ref[:, cols] = ((q1[:, cols] * cos + q2[:, cols] * sin) * scale).astype(BF16)
        kc_ref[:, cols] = (kn[:, cols] + k_rope).astype(BF16)
    vc_ref[...] = _dot(ckvn, wv_ref[...]).astype(BF16)


def _cd_proj(x, g, w, qn, kvn, wq1, wq2, wk, wv, cos_t, sin_t, t):
    n, d = x.shape
    tm = min(ROW_TILE, t)
    nt = t // tm
    full = _const_spec
    consts = [g.reshape(1, d), w, qn.reshape(1, -1), kvn.reshape(1, -1), wq1, wq2, wk, wv]
    out_w = (C_HEADS * LANES, C_HEADS * LANES, C_HEADS * C_V) + CD_PROJ_WIDTHS[4:]
    return pl.pallas_call(
        _cd_proj_kernel,
        grid=(n // tm,),
        in_specs=[pl.BlockSpec((tm, d), lambda i: (i, 0))] + [full(a) for a in consts]
        + [pl.BlockSpec((tm, LANES), lambda i: (i % nt, 0)),
           pl.BlockSpec((tm, LANES), lambda i: (i % nt, 0))],
        out_specs=[pl.BlockSpec((tm, wd), lambda i: (i, 0)) for wd in out_w],
        out_shape=[jax.ShapeDtypeStruct((n, wd), BF16) for wd in out_w],
        compiler_params=_params(("parallel",)),
        name="cd_proj",
    )(x, *consts, cos_t, sin_t)


def _pair_output(o, tq, hd):
    return jnp.concatenate([o[:hd, :tq], o[hd:, tq:]], axis=0).T


def _attn_c_kernel(q_ref, k_ref, v_ref, o_ref, m_ref, l_ref, acc_ref, vt_ref, sa_ref, sb_ref, *, t, tq, tk):
    @pl.when(pl.program_id(2) == 0)
    def _():
        _fill_vt(v_ref, vt_ref, tk)

    q0 = q_ref[:, :LANES]
    q1 = q_ref[:, LANES:]
    _softmax_init(m_ref, l_ref, acc_ref)
    nk = t // tk

    def scores(j):
        rows = pl.ds(pl.multiple_of(j * tk, tk), tk)
        return jnp.concatenate([_dot_nt(k_ref[rows, :LANES], q0), _dot_nt(k_ref[rows, LANES:], q1)],
                               axis=1)

    sa_ref[...] = scores(0)

    def body(jj, carry):
        j = 2 * jj
        sb_ref[...] = scores(j + 1)
        _softmax_step_t(sa_ref[...], vt_ref[j], m_ref, l_ref, acc_ref)
        sa_ref[...] = scores(jnp.minimum(j + 2, nk - 1))
        _softmax_step_t(sb_ref[...], vt_ref[j + 1], m_ref, l_ref, acc_ref)
        return carry

    lax.fori_loop(0, nk // 2, body, 0)
    o_ref[...] = _pair_output(acc_ref[...] / l_ref[...], tq, C_V).astype(o_ref.dtype)


def _attn_c(qc, kc, vc, batch, t):
    tq = min(256, t)
    tk = min(512, t)
    nq = t // tq
    pairs = C_HEADS // 2
    return pl.pallas_call(
        functools.partial(_attn_c_kernel, t=t, tq=tq, tk=tk),
        grid=(batch, pairs, nq),
        in_specs=[pl.BlockSpec((tq, 2 * LANES), lambda b, p, i: (b * nq + i, p)),
                  pl.BlockSpec((t, 2 * LANES), lambda b, p, i: (b, p)),
                  pl.BlockSpec((t, LANES), lambda b, p, i: (b, p))],
        out_specs=pl.BlockSpec((tq, LANES), lambda b, p, i: (b * nq + i, p)),
        out_shape=jax.ShapeDtypeStruct(vc.shape, BF16),
        scratch_shapes=_attn_scratch(2 * tq, LANES, t, tk),
        compiler_params=_params(("parallel", "parallel", "arbitrary")),
        name="attn_c",
    )(qc, kc, vc)


D_REACH = max((w // (2 * dil)) * dil for w, dil in D_PATTERNS)


def _d_tile_range(tq, tk):
    return -((D_REACH + tk) // tq) + 1, (tq + D_REACH) // tq - 1


def _attn_d_kernel(slope_ref, q_ref, k_ref, v_ref, o_ref, m_ref, l_ref, acc_ref, vt_ref, sa_ref, sb_ref,
                   bm_ref, *, t, tq, tk):
    p = pl.program_id(0)
    b = pl.program_id(1)
    i = pl.program_id(2)
    nk = t // tk
    u_lo, u_hi = _d_tile_range(tq, tk)
    masked_tile = u_hi - u_lo + 1

    @pl.when((b == 0) & (i == 0))
    def _():
        s0 = slope_ref[2 * p] * LOG2E
        s1 = slope_ref[2 * p + 1] * LOG2E
        amb = (lax.broadcasted_iota(jnp.int32, (tk, tq), 0)
               - lax.broadcasted_iota(jnp.int32, (tk, tq), 1))
        for u in range(u_lo, u_hi + 1):
            d = amb + u * tq
            ad = jnp.abs(d)
            mult = jnp.zeros((tk, tq), F32)
            for w, dil in D_PATTERNS:
                hit = (ad <= (w // (2 * dil)) * dil) & ((d & (dil - 1)) == 0)
                mult = mult + jnp.where(hit, 1.0, 0.0)
            adf = ad.astype(F32)
            logm = jnp.log2(jnp.maximum(mult, 1.0))
            seen = mult > 0.0
            bm_ref[u - u_lo] = jnp.concatenate([jnp.where(seen, s0 * adf - logm, MASKED),
                                                jnp.where(seen, s1 * adf - logm, MASKED)], axis=1)
        bm_ref[masked_tile] = jnp.full((tk, 2 * tq), MASKED, F32)

    @pl.when(i == 0)
    def _():
        _fill_vt(v_ref, vt_ref, tk)

    q = q_ref[...]
    lane = lax.broadcasted_iota(jnp.int32, (tq, LANES), 1)
    low = lane < HEAD_DIM
    zero = jnp.zeros_like(q)
    qq = jnp.concatenate([jnp.where(low, q, zero), jnp.where(low, zero, q)], axis=0)
    _softmax_init(m_ref, l_ref, acc_ref)
    c_lo = jnp.maximum(i * tq - D_REACH, 0) // tk
    c_hi = jnp.minimum((i * tq + tq + D_REACH + tk - 1) // tk, nk)

    def scores(c):
        cc = jnp.minimum(c, nk - 1)
        tile = jnp.where(c < c_hi, c * (tk // tq) - i - u_lo, masked_tile)
        return _dot_nt(k_ref[pl.ds(pl.multiple_of(cc * tk, tk), tk), :], qq) - bm_ref[tile]

    sa_ref[...] = scores(c_lo)

    def body(jj, carry):
        c = c_lo + 2 * jj
        sb_ref[...] = scores(c + 1)
        _softmax_step_t(sa_ref[...], vt_ref[jnp.minimum(c, nk - 1)], m_ref, l_ref, acc_ref)
        sa_ref[...] = scores(c + 2)
        _softmax_step_t(sb_ref[...], vt_ref[jnp.minimum(c + 1, nk - 1)], m_ref, l_ref, acc_ref)
        return carry

    lax.fori_loop(0, (c_hi - c_lo + 1) // 2, body, 0)
    o_ref[...] = _pair_output(acc_ref[...] / l_ref[...], tq, HEAD_DIM).astype(o_ref.dtype)


def _attn_d(qd, kd, vd, batch, t):
    tq = min(256, t)
    tk = min(512, t)
    nq = t // tq
    pairs = D_HEADS // 2
    slopes = jnp.asarray(_alibi_slopes(D_HEADS), F32)
    u_lo, u_hi = _d_tile_range(tq, tk)
    return pl.pallas_call(
        functools.partial(_attn_d_kernel, t=t, tq=tq, tk=tk),
        grid=(pairs, batch, nq),
        in_specs=[pl.BlockSpec(memory_space=pltpu.SMEM),
                  pl.BlockSpec((tq, LANES), lambda p, b, i: (b * nq + i, p)),
                  pl.BlockSpec((t, LANES), lambda p, b, i: (b, p)),
                  pl.BlockSpec((t, LANES), lambda p, b, i: (b, p))],
        out_specs=pl.BlockSpec((tq, LANES), lambda p, b, i: (b * nq + i, p)),
        out_shape=jax.ShapeDtypeStruct(qd.shape, BF16),
        scratch_shapes=_attn_scratch(2 * tq, LANES, t, tk)
        + [pltpu.VMEM((u_hi - u_lo + 2, tk, 2 * tq), F32)],
        compiler_params=_params(("arbitrary", "arbitrary", "arbitrary")),
        name="attn_d",
    )(slopes, qd, kd, vd)


def _router_kernel(x_ref, g_ref, rt_ref, xn_ref, ids_ref, gates_ref):
    xn = _rms(x_ref[...], g_ref[...])
    xn_ref[...] = xn
    x_hi = xn.astype(BF16)
    x_lo = (xn - x_hi.astype(F32)).astype(BF16)
    rt = rt_ref[...]
    r_hi = rt.astype(BF16)
    r_lo = (rt - r_hi.astype(F32)).astype(BF16)
    logits = _dot_nt(r_hi, x_hi) + _dot_nt(r_hi, x_lo) + _dot_nt(r_lo, x_hi)
    eid = lax.broadcasted_iota(jnp.int32, logits.shape, 0)
    v1 = jnp.max(logits, axis=0, keepdims=True)
    i1 = jnp.min(jnp.where(logits == v1, eid, N_EXPERTS), axis=0, keepdims=True)
    rest = jnp.where(eid == i1, -jnp.inf, logits)
    v2 = jnp.max(rest, axis=0, keepdims=True)
    i2 = jnp.min(jnp.where(rest == v2, eid, N_EXPERTS), axis=0, keepdims=True)
    e2 = jnp.exp(v2 - v1)
    g1 = 1.0 / (1.0 + e2)
    ids_ref[...] = jnp.concatenate([i1, i2], axis=0)
    gates_ref[...] = jnp.concatenate([g1, e2 * g1], axis=0)


def _router(x, g, router_t):
    n, d = x.shape
    tm = min(ROW_TILE, n)
    return pl.pallas_call(
        _router_kernel,
        grid=(n // tm,),
        in_specs=[pl.BlockSpec((tm, d), lambda i: (i, 0)),
                  pl.BlockSpec((1, d), lambda i: (0, 0)),
                  pl.BlockSpec(router_t.shape, lambda i: (0, 0))],
        out_specs=[pl.BlockSpec((tm, d), lambda i: (i, 0)),
                   pl.BlockSpec((2, tm), lambda i: (0, i)),
                   pl.BlockSpec((2, tm), lambda i: (0, i))],
        out_shape=[jax.ShapeDtypeStruct((n, d), F32),
                   jax.ShapeDtypeStruct((2, n), jnp.int32),
                   jax.ShapeDtypeStruct((2, n), F32)],
        compiler_params=_params(("parallel",)),
        name="router",
    )(x, g.reshape(1, d), router_t)


def _moe_kernel(te_ref, nv_ref, idx_hbm, x_hbm, wg_ref, wu_ref, wd_ref, o_ref,
                idx_a, idx_b, xbuf_ref, xb_ref, acc_ref, isem, rsem):
    ti = pl.program_id(0)
    f = pl.program_id(1)
    nv = nv_ref[0]
    live = ti < nv
    tm = xb_ref.shape[0]
    idx_slots = (idx_a, idx_b)

    def idx_copy(tile, slot):
        return pltpu.make_async_copy(idx_hbm.at[pl.ds(pl.multiple_of(tile * tm, tm), tm)],
                                     idx_slots[slot], isem.at[slot])

    def row_copy(tok, r, slot):
        return pltpu.make_async_copy(x_hbm.at[pl.ds(tok, 1)], xbuf_ref.at[slot, pl.ds(r, 1)],
                                     rsem.at[slot])

    def issue_rows(slot):
        def row(r, carry):
            row_copy(idx_slots[slot][r], r, slot).start()
            return carry
        lax.fori_loop(0, tm, row, 0, unroll=8)

    def wait_rows(slot):
        def row(r, carry):
            row_copy(0, 0, slot).wait()
            return carry
        lax.fori_loop(0, tm, row, 0, unroll=8)

    def start_tile(slot):
        wait_rows(slot)

        @pl.when(ti + 1 < nv)
        def _():
            idx_copy(ti + 1, 1 - slot).wait()
            issue_rows(1 - slot)

        @pl.when(ti + 2 < nv)
        def _():
            idx_copy(ti + 2, slot).start()

        xb_ref[...] = xbuf_ref[slot].astype(BF16)
        acc_ref[...] = jnp.zeros(acc_ref.shape, F32)

    @pl.when((ti == 0) & (f == 0))
    def _():
        idx_copy(0, 0).start()
        idx_copy(0, 0).wait()
        issue_rows(0)

        @pl.when(1 < nv)
        def _():
            idx_copy(1, 1).start()

    for slot in range(2):
        @pl.when(live & (f == 0) & (ti % 2 == slot))
        def _(slot=slot):
            start_tile(slot)

    @pl.when(live)
    def _():
        xb = xb_ref[...]
        hg = _dot(xb, wg_ref[...])
        hu = _dot(xb, wu_ref[...])
        act = (hg / (1.0 + jnp.exp(-hg)) * hu).astype(BF16)
        acc_ref[...] += _dot(act, wd_ref[...])

    @pl.when(f == pl.num_programs(1) - 1)
    def _():
        o_ref[...] = jnp.where(live, acc_ref[...], 0.0)


def _moe(tile_expert, n_valid, src_tok, xn, wg, wu, wd):
    p = src_tok.shape[0]
    d = xn.shape[1]
    tm = MOE_TILE
    fc = MOE_FCHUNK
    nf = wg.shape[2] // fc

    def tile(ti, nv):
        return jnp.minimum(ti, nv[0] - 1)

    grid_spec = pltpu.PrefetchScalarGridSpec(
        num_scalar_prefetch=2,
        grid=(p // tm, nf),
        in_specs=[pl.BlockSpec(memory_space=pl.ANY), pl.BlockSpec(memory_space=pl.ANY),
                  pl.BlockSpec((None, d, fc), lambda ti, f, te, nv: (te[tile(ti, nv)], 0, jnp.where(ti < nv[0], f, nf - 1))),
                  pl.BlockSpec((None, d, fc), lambda ti, f, te, nv: (te[tile(ti, nv)], 0, jnp.where(ti < nv[0], f, nf - 1))),
                  pl.BlockSpec((None, fc, d), lambda ti, f, te, nv: (te[tile(ti, nv)], jnp.where(ti < nv[0], f, nf - 1), 0))],
        out_specs=pl.BlockSpec((tm, d), lambda ti, f, te, nv: (ti, 0)),
        scratch_shapes=[pltpu.SMEM((tm,), jnp.int32), pltpu.SMEM((tm,), jnp.int32),
                        pltpu.VMEM((2, tm, d), F32), pltpu.VMEM((tm, d), BF16), pltpu.VMEM((tm, d), F32),
                        pltpu.SemaphoreType.DMA((2,)), pltpu.SemaphoreType.DMA((2,))],
    )
    return pl.pallas_call(
        _moe_kernel,
        grid_spec=grid_spec,
        out_shape=jax.ShapeDtypeStruct((p, d), F32),
        compiler_params=_params(("arbitrary", "arbitrary")),
        name="moe_experts",
    )(tile_expert, n_valid, src_tok, xn, wg, wu, wd)


def _combine_kernel(pos_hbm, x_ref, gate_ref, g_ref, ys_hbm, o_ref, idx_a, idx_b, buf_ref, isem, rsem,
                    *, rt, step0, nsteps):
    s = pl.program_id(0)
    idx_slots = (idx_a, idx_b)

    def idx_copy(step, slot):
        start = pl.multiple_of((step + step0) * 2 * rt, 2 * rt)
        return pltpu.make_async_copy(pos_hbm.at[pl.ds(start, 2 * rt)], idx_slots[slot], isem.at[slot])

    def row_copy(src, k, r, slot):
        return pltpu.make_async_copy(ys_hbm.at[pl.ds(src, 1)], buf_ref.at[slot, k, pl.ds(r, 1)],
                                     rsem.at[slot])

    def issue_rows(slot):
        def row(j, carry):
            for k in range(2):
                row_copy(idx_slots[slot][k * rt + j], k, j, slot).start()
            return carry
        lax.fori_loop(0, rt, row, 0, unroll=8)

    def wait_rows(slot):
        def row(j, carry):
            for k in range(2):
                row_copy(0, k, 0, slot).wait()
            return carry
        lax.fori_loop(0, rt, row, 0, unroll=8)

    @pl.when(s == 0)
    def _():
        idx_copy(0, 0).start()
        idx_copy(0, 0).wait()
        issue_rows(0)
        if nsteps > 1:
            idx_copy(1, 1).start()

    def step(slot):
        @pl.when(s + 1 < nsteps)
        def _():
            idx_copy(s + 1, 1 - slot).wait()
            issue_rows(1 - slot)

        @pl.when(s + 2 < nsteps)
        def _():
            idx_copy(s + 2, slot).start()

        wait_rows(slot)
        gates = gate_ref[...]
        y = x_ref[...] + gates[:, 0:1] * buf_ref[slot, 0] + gates[:, 1:2] * buf_ref[slot, 1]
        o_ref[...] = _rms(y, g_ref[...])

    for slot in range(2):
        @pl.when(s % 2 == slot)
        def _(slot=slot):
            step(slot)


def _combine(pos_flat, x, gates_t, g, ys, row0, nrows):
    d = x.shape[1]
    rt = GATHER_CHUNK // 2
    step0 = row0 // rt
    nsteps = nrows // rt
    return pl.pallas_call(
        functools.partial(_combine_kernel, rt=rt, step0=step0, nsteps=nsteps),
        grid=(nsteps,),
        in_specs=[pl.BlockSpec(memory_space=pl.ANY),
                  pl.BlockSpec((rt, d), lambda i: (i + step0, 0)),
                  pl.BlockSpec((rt, 2), lambda i: (i + step0, 0)),
                  pl.BlockSpec((1, d), lambda i: (0, 0)),
                  pl.BlockSpec(memory_space=pl.ANY)],
        out_specs=pl.BlockSpec((rt, d), lambda i: (i, 0)),
        out_shape=jax.ShapeDtypeStruct((nrows, d), F32),
        scratch_shapes=[pltpu.SMEM((2 * rt,), jnp.int32), pltpu.SMEM((2 * rt,), jnp.int32),
                        pltpu.VMEM((2, 2, rt, d), F32),
                        pltpu.SemaphoreType.DMA((2,)), pltpu.SemaphoreType.DMA((2,))],
        compiler_params=_params(("arbitrary",)),
        name="combine",
    )(pos_flat, x, gates_t, g.reshape(1, d), ys)


def _pad_cols(w, lo, total):
    return jnp.pad(w, ((0, 0), (lo, total - lo - w.shape[1])))


def _rot_half_cols(w):
    half = w.shape[1] // 2
    return jnp.concatenate([-w[:, half:], w[:, :half]], axis=1)


def _a_head_perm():
    half = A_HEADS // 2
    cols = []
    for p in range(half):
        cols += list(range(HEAD_DIM * p, HEAD_DIM * (p + 1)))
        cols += list(range(HEAD_DIM * (p + half), HEAD_DIM * (p + half + 1)))
    return np.asarray(cols, np.int32)


def _rope_tables(t):
    half = C_ROPE // 2
    inv = (ROPE_THETA ** (-np.arange(half) * 2.0 / C_ROPE)).astype(np.float32).astype(np.float64)
    ang = np.arange(t, dtype=np.float64)[:, None] * inv[None, :]
    cos = np.concatenate([np.cos(ang), np.cos(ang)], axis=1)
    sin = np.concatenate([np.sin(ang), np.sin(ang)], axis=1)
    pad = LANES - C_NOPE - C_ROPE
    cos_t = np.concatenate([np.ones((t, C_NOPE)), cos, np.zeros((t, pad))], axis=1)
    sin_t = np.concatenate([np.zeros((t, C_NOPE)), sin, np.zeros((t, pad))], axis=1)
    return jnp.asarray(cos_t, F32), jnp.asarray(sin_t, F32)


def _layer_ab(x, batch, t, norm_attn, w_in, sink, lq1, lk1, lq2, lk2, subln, w_out, norm_ffn,
              w_gate, w_up, w_down, lam_init):
    perm = _a_head_perm()
    qa_w = A_HEADS * HEAD_DIM
    w_in_k = jnp.concatenate([w_in[:, :qa_w][:, perm], w_in[:, qa_w:]], axis=1).astype(BF16)
    scale = HEAD_DIM ** -0.5
    qa, ka, va, qd, kd, vd = _norm_proj(x, norm_attn, w_in_k, AB_WIDTHS,
                                        (scale, 1.0, 1.0, scale * LOG2E, 1.0, 1.0))
    oa = _attn_a(sink, qa, ka, va, batch, t)
    od = _attn_b(lq1, lk1, lq2, lk2, subln, qd, kd, vd, batch, t, lam_init)
    x = _out_proj(x, oa, od, w_out[:qa_w][perm].astype(BF16), w_out[qa_w:].astype(BF16))
    nchunks = w_gate.shape[1] // FFN_CHUNK
    to_chunks = lambda w: w.reshape(w.shape[0], nchunks, FFN_CHUNK).transpose(1, 0, 2).astype(BF16)
    return _ffn(x, norm_ffn, to_chunks(w_gate), to_chunks(w_up), w_down.astype(BF16))


def _cd_weights(w_in, w_uq, w_ukv):
    o = np.cumsum((0,) + CD_WIDTHS)
    w_kr = w_in[:, o[2]:o[3]]
    w_proj = jnp.concatenate([w_in[:, :o[2]], _pad_cols(w_kr, C_NOPE, LANES),
                              _pad_cols(_rot_half_cols(w_kr), C_NOPE, LANES), w_in[:, o[3]:]], axis=1)
    dq = C_NOPE + C_ROPE
    wq1, wq2, wk, wv = [], [], [], []
    for h in range(C_HEADS):
        wq_h = w_uq[:, dq * h:dq * (h + 1)]
        wq1.append(_pad_cols(wq_h, 0, LANES))
        wq2.append(_pad_cols(_rot_half_cols(wq_h[:, C_NOPE:]), C_NOPE, LANES))
        wkv_h = w_ukv[:, (C_NOPE + C_V) * h:(C_NOPE + C_V) * (h + 1)]
        wk.append(_pad_cols(wkv_h[:, :C_NOPE], 0, LANES))
        wv.append(wkv_h[:, C_NOPE:])
    cat = lambda ws: jnp.concatenate(ws, axis=1).astype(BF16)
    return w_proj.astype(BF16), cat(wq1), cat(wq2), cat(wk), cat(wv)


def _route(ids, n):
    e = ids.reshape(-1)
    onehot = (e[:, None] == jnp.arange(N_EXPERTS, dtype=jnp.int32)[None, :]).astype(jnp.int32)
    csum = jnp.cumsum(onehot, axis=0)
    rank = jnp.sum(onehot * (csum - 1), axis=1)
    counts = csum[-1]
    padded = ((counts + MOE_TILE - 1) // MOE_TILE) * MOE_TILE
    gend = jnp.cumsum(padded)
    gstart = gend - padded
    slot = gstart[e] + rank
    p_total = 2 * n + N_EXPERTS * MOE_TILE
    tok = jnp.tile(jnp.arange(n, dtype=jnp.int32), 2)
    src_tok = jnp.zeros((p_total,), jnp.int32).at[slot].set(tok)
    ntiles = p_total // MOE_TILE
    tile_start = jnp.arange(ntiles, dtype=jnp.int32) * MOE_TILE
    tile_expert = jnp.minimum(jnp.sum((tile_start[:, None] >= gend[None, :]).astype(jnp.int32), axis=1),
                              N_EXPERTS - 1).astype(jnp.int32)
    n_valid = (gend[-1] // MOE_TILE).astype(jnp.int32).reshape(1)
    return slot.astype(jnp.int32), src_tok, tile_expert, n_valid


def _layer_cd(x, batch, t, norm_attn, w_in, q_norm, kv_norm, w_uq, w_ukv, w_out, norm_ffn, router,
              w_gate, w_up, w_down, final_norm, splits):
    n = x.shape[0]
    w_proj, wq1, wq2, wk, wv = _cd_weights(w_in, w_uq, w_ukv)
    cos_t, sin_t = _rope_tables(t)
    qc, kc, vc, qd, kd, vd = _cd_proj(x, norm_attn, w_proj, q_norm, kv_norm, wq1, wq2, wk, wv,
                                      cos_t, sin_t, t)
    oc = _attn_c(qc, kc, vc, batch, t)
    od = _attn_d(qd, kd, vd, batch, t)
    cw = C_HEADS * C_V
    x = _out_proj(x, oc, od, w_out[:cw].astype(BF16), w_out[cw:].astype(BF16))
    xn, ids, gates = _router(x, norm_ffn, router.T)
    slot, src_tok, tile_expert, n_valid = _route(ids, n)
    ys = _moe(tile_expert, n_valid, src_tok, xn, w_gate.astype(BF16), w_up.astype(BF16),
              w_down.astype(BF16))
    rt = GATHER_CHUNK // 2
    pos_flat = slot.reshape(2, n // rt, rt).transpose(1, 0, 2).reshape(-1)
    gates_t = gates.T
    outs = []
    row0 = 0
    for rows in splits:
        outs.append(_combine(pos_flat, x, gates_t, final_norm, ys, row0, rows))
        row0 += rows
    return outs


def kernel(x_prompt, x_sample, ab_norm_attn, ab_w_in, ab_sink, ab_lambda_q1, ab_lambda_k1, ab_lambda_q2, ab_lambda_k2, ab_subln, ab_w_out, ab_norm_ffn, ffn_w_gate, ffn_w_up, ffn_w_down, cd_norm_attn, cd_w_in, cd_q_norm, cd_kv_norm, cd_w_uq, cd_w_ukv, cd_w_out, cd_norm_ffn, moe_router, moe_w_gate, moe_w_up, moe_w_down, final_norm):
    bp, t, d = x_prompt.shape
    bs = x_sample.shape[0]
    assert x_sample.shape[1] == t
    batch = bp + bs
    x = jnp.concatenate([x_prompt.reshape(bp * t, d), x_sample.reshape(bs * t, d)], axis=0)
    lam_init = 0.8 - 0.6 * math.exp(-0.3 * 0)
    x = _layer_ab(x, batch, t, ab_norm_attn[0], ab_w_in[0], ab_sink[0], ab_lambda_q1[0],
                  ab_lambda_k1[0], ab_lambda_q2[0], ab_lambda_k2[0], ab_subln[0], ab_w_out[0],
                  ab_norm_ffn[0], ffn_w_gate[0], ffn_w_up[0], ffn_w_down[0], lam_init)
    y_prompt, y_sample = _layer_cd(x, batch, t, cd_norm_attn[0], cd_w_in[0], cd_q_norm[0],
                                   cd_kv_norm[0], cd_w_uq[0], cd_w_ukv[0], cd_w_out[0],
                                   cd_norm_ffn[0], moe_router[0], moe_w_gate[0], moe_w_up[0],
                                   moe_w_down[0], final_norm, (bp * t, bs * t))
    return (y_prompt.reshape(bp, t, d), y_sample.reshape(bs, t, d))
```

```python
import functools
import math

import numpy as np
import jax
import jax.numpy as jnp
from jax import lax
from jax.experimental import pallas as pl
from jax.experimental.pallas import tpu as pltpu

F32 = jnp.float32
BF16 = jnp.bfloat16

D_MODEL = 1024
HEAD_DIM = 64
A_HEADS = 8
A_KV_HEADS = 2
A_WINDOW = 128
A_BLOCK = 128
B_HEADS = 4
B_V_DIM = 2 * HEAD_DIM
C_HEADS = 8
C_Q_RANK = 384
C_KV_RANK = 256
C_NOPE = 64
C_ROPE = 32
C_V = 64
D_HEADS = 8
D_PATTERNS = ((128, 1), (512, 4), (2048, 16))
FFN_DIM = 2816
N_EXPERTS = 8
EXPERT_DIM = 3584
ROPE_THETA = 10000.0
RMS_EPS = 1e-6
NEG_INF = -1e30
LOG2E = math.log2(math.e)
LANES = 128
VMEM_LIMIT = 56 * 1024 * 1024

AB_WIDTHS = (A_HEADS * HEAD_DIM, A_KV_HEADS * HEAD_DIM, A_KV_HEADS * HEAD_DIM,
             B_HEADS * 2 * HEAD_DIM, B_HEADS * 2 * HEAD_DIM, B_HEADS * B_V_DIM)
CD_WIDTHS = (C_Q_RANK, C_KV_RANK, C_ROPE,
             D_HEADS * HEAD_DIM, D_HEADS * HEAD_DIM, D_HEADS * HEAD_DIM)
CD_PROJ_WIDTHS = (C_Q_RANK, C_KV_RANK, LANES, LANES,
                  D_HEADS * HEAD_DIM, D_HEADS * HEAD_DIM, D_HEADS * HEAD_DIM)

ROW_TILE = 512
MOE_TILE = 1024
MOE_FCHUNK = 512
GATHER_CHUNK = 1024
FFN_CHUNK = 256


def _alibi_slopes(n):
    return [float(2.0 ** (-8.0 * (i + 1) / n)) for i in range(n)]


def _params(sem, vmem=VMEM_LIMIT):
    return pltpu.CompilerParams(dimension_semantics=sem, vmem_limit_bytes=vmem)


def _const_spec(a):
    return pl.BlockSpec(a.shape, lambda *_: (0,) * a.ndim, pipeline_mode=pl.Buffered(1))


def _rms(x, g):
    ms = jnp.mean(x * x, axis=-1, keepdims=True)
    return x * lax.rsqrt(ms + RMS_EPS) * g


def _dot(a, b):
    return jnp.dot(a, b, preferred_element_type=F32)


def _dot_nt(a, b):
    return lax.dot_general(a, b, (((1,), (1,)), ((), ())), preferred_element_type=F32)


def _norm_proj_kernel(x_ref, g_ref, w_ref, *o_refs, widths, scales):
    xn = _rms(x_ref[...], g_ref[...]).astype(BF16)
    off = 0
    for o_ref, w, sc in zip(o_refs, widths, scales):
        y = _dot(xn, w_ref[:, off:off + w])
        if sc != 1.0:
            y = y * sc
        o_ref[...] = y.astype(o_ref.dtype)
        off += w


def _norm_proj(x, g, w, widths, scales):
    n, d = x.shape
    tm = min(ROW_TILE, n)
    return pl.pallas_call(
        functools.partial(_norm_proj_kernel, widths=widths, scales=scales),
        grid=(n // tm,),
        in_specs=[pl.BlockSpec((tm, d), lambda i: (i, 0)),
                  pl.BlockSpec((1, d), lambda i: (0, 0)),
                  pl.BlockSpec(w.shape, lambda i: (0, 0))],
        out_specs=[pl.BlockSpec((tm, wd), lambda i: (i, 0)) for wd in widths],
        out_shape=[jax.ShapeDtypeStruct((n, wd), BF16) for wd in widths],
        compiler_params=_params(("parallel",)),
        name="norm_proj",
    )(x, g.reshape(1, d), w)


def _attn_a_kernel(sink_ref, q_ref, kp_ref, kc_ref, kn_ref, vp_ref, vc_ref, vn_ref, o_ref, *, nb):
    i = pl.program_id(1)
    blk = A_BLOCK
    half_heads = A_HEADS // 2
    kcat = jnp.concatenate([kp_ref[...], kc_ref[...], kn_ref[...]], axis=0)
    vcat = jnp.concatenate([vp_ref[...], vc_ref[...], vn_ref[...]], axis=0)
    vt = vcat.astype(F32).T.astype(BF16)
    lane = lax.broadcasted_iota(jnp.int32, (blk, LANES), 1)
    low = lane < HEAD_DIM
    parts = []
    for half in range(2):
        for p in range(half_heads):
            qp = q_ref[:, LANES * p:LANES * (p + 1)]
            parts.append(jnp.where(low if half == 0 else jnp.logical_not(low), qp, jnp.zeros_like(qp)))
    st = _dot_nt(kcat, jnp.concatenate(parts, axis=0))
    key = lax.broadcasted_iota(jnp.int32, (3 * blk, blk), 0)
    qry = lax.broadcasted_iota(jnp.int32, (3 * blk, blk), 1)
    dist = jnp.abs(key - blk - qry)
    valid = (dist <= A_WINDOW) & ((key >= blk) | (i > 0)) & ((key < 2 * blk) | (i < nb - 1))
    distf = dist.astype(F32)
    slopes = _alibi_slopes(A_HEADS)
    probs, scales = [], []
    for h in range(A_HEADS):
        sh = jnp.where(valid, st[:, blk * h:blk * (h + 1)] - (slopes[h] * LOG2E) * distf, NEG_INF)
        m = jnp.max(sh, axis=0, keepdims=True)
        e = jnp.exp2(sh - m)
        l = jnp.sum(e, axis=0, keepdims=True)
        lse = m * (1.0 / LOG2E) + jnp.log(l)
        gate = 1.0 / (1.0 + jnp.exp(sink_ref[h] - lse))
        probs.append(e.astype(BF16))
        scales.append(gate / l)
    o = _dot(vt, jnp.concatenate(probs, axis=1)) * jnp.concatenate(scales, axis=1)
    for p in range(half_heads):
        top = o[:HEAD_DIM, blk * p:blk * (p + 1)]
        bot = o[HEAD_DIM:, blk * (p + half_heads):blk * (p + half_heads + 1)]
        o_ref[:, LANES * p:LANES * (p + 1)] = jnp.concatenate([top, bot], axis=0).T.astype(o_ref.dtype)


def _attn_a(sink, qa, ka, va, batch, t):
    nb = t // A_BLOCK
    qw = qa.shape[1]
    kw = ka.shape[1]

    def nbr(delta):
        return lambda b, i: (b * nb + jnp.clip(i + delta, 0, nb - 1), 0)

    kv_specs = [pl.BlockSpec((A_BLOCK, kw), nbr(d)) for d in (-1, 0, 1)]
    return pl.pallas_call(
        functools.partial(_attn_a_kernel, nb=nb),
        grid=(batch, nb),
        in_specs=[pl.BlockSpec(memory_space=pltpu.SMEM),
                  pl.BlockSpec((A_BLOCK, qw), lambda b, i: (b * nb + i, 0))] + kv_specs + kv_specs,
        out_specs=pl.BlockSpec((A_BLOCK, qw), lambda b, i: (b * nb + i, 0)),
        out_shape=jax.ShapeDtypeStruct(qa.shape, BF16),
        compiler_params=_params(("parallel", "parallel")),
        name="attn_a",
    )(sink, qa, ka, ka, ka, va, va, va)


def _softmax_step_t(st, vt, m_ref, l_ref, acc_ref, offset=None):
    m_old = m_ref[...]
    smax = jnp.max(st, axis=0, keepdims=True)
    if offset is None:
        m_new = jnp.maximum(m_old, smax)
        shift = m_new
    else:
        m_new = jnp.maximum(m_old, smax - offset)
        shift = m_new + offset
    alpha = jnp.exp2(m_old - m_new)
    p = jnp.exp2(st - shift)
    l_ref[...] = alpha * l_ref[...] + jnp.sum(p, axis=0, keepdims=True)
    acc_ref[...] = alpha * acc_ref[...] + _dot(vt, p.astype(BF16))
    m_ref[...] = m_new


def _attention_chunks(refs, first, n, scores, vt_at, offset=None):
    m_ref, l_ref, acc_ref, sa_ref, sb_ref = refs
    bufs = (sa_ref, sb_ref)
    _softmax_init(m_ref, l_ref, acc_ref)
    sa_ref[...] = scores(first)
    for k in range(n):
        if k + 1 < n:
            bufs[(k + 1) % 2][...] = scores(first + k + 1)
        _softmax_step_t(bufs[k % 2][...], vt_at(first + k), m_ref, l_ref, acc_ref,
                        None if offset is None else offset(first + k))


MASKED = 1e30
M_INIT = -1e29


def _softmax_init(m_ref, l_ref, acc_ref):
    m_ref[...] = jnp.full(m_ref.shape, M_INIT, F32)
    l_ref[...] = jnp.zeros(l_ref.shape, F32)
    acc_ref[...] = jnp.zeros(acc_ref.shape, F32)


def _fill_vt(v_ref, vt_ref, tk):
    for c in range(vt_ref.shape[0]):
        vt_ref[c] = v_ref[c * tk:(c + 1) * tk, :].astype(F32).T.astype(vt_ref.dtype)


def _attn_scratch(lanes, vdim, t, tk):
    return [pltpu.VMEM((t // tk, vdim, tk), BF16),
            pltpu.VMEM((1, lanes), F32), pltpu.VMEM((1, lanes), F32), pltpu.VMEM((LANES, lanes), F32),
            pltpu.VMEM((tk, lanes), F32), pltpu.VMEM((tk, lanes), F32)]


def _attn_b_kernel(lq1_ref, lk1_ref, lq2_ref, lk2_ref, subln_ref, q_ref, k_ref, v_ref, o_ref,
                   vt_ref, *rest, lam_init, t, tq, tk):
    refs, bias_ref = rest[:-1], rest[-1]
    acc_ref, l_ref = refs[2], refs[1]
    i = pl.program_id(1)
    r = tk // tq
    nk = t // tk
    slopes = [s * LOG2E for s in _alibi_slopes(B_HEADS)]

    @pl.when(i == 0)
    def _():
        _fill_vt(v_ref, vt_ref, tk)
        amb = (lax.broadcasted_iota(jnp.int32, (tk, tq), 0)
               - lax.broadcasted_iota(jnp.int32, (tk, tq), 1))
        for h in range(B_HEADS):
            lin = slopes[h] * amb.astype(F32)
            bias_ref[h, 0] = lin
            bias_ref[h, 1] = -lin
            for s in range(r):
                bias_ref[h, 2 + s] = slopes[h] * jnp.abs(amb - s * tq).astype(F32)

    lam = (jnp.exp(jnp.sum(lq1_ref[...] * lk1_ref[...], axis=-1, keepdims=True))
           - jnp.exp(jnp.sum(lq2_ref[...] * lk2_ref[...], axis=-1, keepdims=True)) + lam_init)
    lane = lax.broadcasted_iota(jnp.int32, (tq, LANES), 1)
    low = lane < HEAD_DIM
    jd = i // r
    for h in range(B_HEADS):
        cols = slice(LANES * h, LANES * (h + 1))
        qh = q_ref[:, cols]
        zero = jnp.zeros_like(qh)
        qq = jnp.concatenate([jnp.where(low, qh, zero), jnp.where(low, zero, qh)], axis=0)

        def scores(j, h=h, cols=cols, qq=qq):
            kj = k_ref[j * tk:(j + 1) * tk, cols]
            sel = jnp.where(j > jd, 0, jnp.where(j < jd, 1, 2 + i % r))
            bias = bias_ref[h, sel]
            return _dot_nt(kj, qq) - jnp.concatenate([bias, bias], axis=1)

        def offset(j, slope=slopes[h]):
            sign = jnp.where(j > jd, 1.0, jnp.where(j < jd, -1.0, 0.0))
            return sign * slope * (j * tk - i * tq).astype(F32)

        _attention_chunks(refs, 0, nk, scores, lambda j, cols=cols: vt_ref[j, cols, :], offset)
        o = acc_ref[...] / l_ref[...]
        od = o[:, :tq] - lam * o[:, tq:]
        ms = jnp.mean(od * od, axis=0, keepdims=True)
        y = (od * lax.rsqrt(ms + RMS_EPS)).T * subln_ref[...] * (1.0 - lam_init)
        o_ref[:, cols] = y.astype(o_ref.dtype)


def _attn_b(lq1, lk1, lq2, lk2, subln, qd, kd, vd, batch, t, lam_init):
    tq = min(256, t)
    tk = min(512, t)
    nq = t // tq
    w = qd.shape[1]
    small = lambda a: pl.BlockSpec((1, a.shape[-1]), lambda b, i: (0, 0))
    vecs = [a.reshape(1, -1) for a in (lq1, lk1, lq2, lk2, subln)]
    return pl.pallas_call(
        functools.partial(_attn_b_kernel, lam_init=lam_init, t=t, tq=tq, tk=tk),
        grid=(batch, nq),
        in_specs=[small(a) for a in vecs]
        + [pl.BlockSpec((tq, w), lambda b, i: (b * nq + i, 0)),
           pl.BlockSpec((t, w), lambda b, i: (b, 0)),
           pl.BlockSpec((t, w), lambda b, i: (b, 0))],
        out_specs=pl.BlockSpec((tq, w), lambda b, i: (b * nq + i, 0)),
        out_shape=jax.ShapeDtypeStruct(qd.shape, BF16),
        scratch_shapes=_attn_scratch(2 * tq, w, t, tk)
        + [pltpu.VMEM((B_HEADS, 2 + tk // tq, tk, tq), F32)],
        compiler_params=_params(("parallel", "arbitrary")),
        name="attn_b",
    )(*vecs, qd, kd, vd)


def _out_proj_kernel(x_ref, a1_ref, a2_ref, w1_ref, w2_ref, o_ref):
    o_ref[...] = x_ref[...] + _dot(a1_ref[...], w1_ref[...]) + _dot(a2_ref[...], w2_ref[...])


def _out_proj(x, a1, a2, w1, w2):
    n, d = x.shape
    tm = min(ROW_TILE, n)
    return pl.pallas_call(
        _out_proj_kernel,
        grid=(n // tm,),
        in_specs=[pl.BlockSpec((tm, d), lambda i: (i, 0)),
                  pl.BlockSpec((tm, a1.shape[1]), lambda i: (i, 0)),
                  pl.BlockSpec((tm, a2.shape[1]), lambda i: (i, 0)),
                  pl.BlockSpec(w1.shape, lambda i: (0, 0)),
                  pl.BlockSpec(w2.shape, lambda i: (0, 0))],
        out_specs=pl.BlockSpec((tm, d), lambda i: (i, 0)),
        out_shape=jax.ShapeDtypeStruct((n, d), F32),
        compiler_params=_params(("parallel",)),
        name="out_proj",
    )(x, a1, a2, w1, w2)


def _ffn_kernel(x_ref, g_ref, wg_ref, wu_ref, wd_ref, o_ref, acc_ref, *, nchunks, fc):
    x = x_ref[...]
    xn = _rms(x, g_ref[...]).astype(BF16)
    acc_ref[...] = x

    def body(c, carry):
        hg = _dot(xn, wg_ref[c])
        hu = _dot(xn, wu_ref[c])
        act = (hg / (1.0 + jnp.exp(-hg)) * hu).astype(BF16)
        acc_ref[...] += _dot(act, wd_ref[pl.ds(pl.multiple_of(c * fc, fc), fc), :])
        return carry

    lax.fori_loop(0, nchunks, body, 0)
    o_ref[...] = acc_ref[...]


def _ffn(x, g, wg3, wu3, wd):
    n, d = x.shape
    nchunks, _, fc = wg3.shape
    tm = min(ROW_TILE, n)
    return pl.pallas_call(
        functools.partial(_ffn_kernel, nchunks=nchunks, fc=fc),
        grid=(n // tm,),
        in_specs=[pl.BlockSpec((tm, d), lambda i: (i, 0)),
                  pl.BlockSpec((1, d), lambda i: (0, 0)),
                  _const_spec(wg3), _const_spec(wu3), _const_spec(wd)],
        out_specs=pl.BlockSpec((tm, d), lambda i: (i, 0)),
        out_shape=jax.ShapeDtypeStruct((n, d), F32),
        scratch_shapes=[pltpu.VMEM((tm, d), F32)],
        compiler_params=_params(("parallel",)),
        name="ffn",
    )(x, g.reshape(1, d), wg3, wu3, wd)


def _cd_proj_kernel(x_ref, g_ref, w_ref, qn_ref, kvn_ref, wq1_ref, wq2_ref, wk_ref, wv_ref,
                    cos_ref, sin_ref, qc_ref, kc_ref, vc_ref, qd_ref, kd_ref, vd_ref):
    h = _rms(x_ref[...], g_ref[...]).astype(BF16)
    offs = np.cumsum((0,) + CD_PROJ_WIDTHS)
    seg = lambda k: _dot(h, w_ref[:, int(offs[k]):int(offs[k + 1])])
    cq, ckv, kr_plain, kr_rot = seg(0), seg(1), seg(2), seg(3)
    qd_ref[...] = (seg(4) * (HEAD_DIM ** -0.5 * LOG2E)).astype(BF16)
    kd_ref[...] = seg(5).astype(BF16)
    vd_ref[...] = seg(6).astype(BF16)
    cqn = _rms(cq, qn_ref[...]).astype(BF16)
    ckvn = _rms(ckv, kvn_ref[...]).astype(BF16)
    cos = cos_ref[...]
    sin = sin_ref[...]
    k_rope = kr_plain * cos + kr_rot * sin
    q1 = _dot(cqn, wq1_ref[...])
    q2 = _dot(cqn, wq2_ref[...])
    kn = _dot(ckvn, wk_ref[...])
    scale = (C_NOPE + C_ROPE) ** -0.5 * LOG2E
    for hh in range(C_HEADS):
        cols = slice(LANES * hh, LANES * (hh + 1))
        qc_ref[:, cols] = ((q1[:, cols] * cos + q2[:, cols] * sin) * scale).astype(BF16)
        kc_ref[:, cols] = (kn[:, cols] + k_rope).astype(BF16)
    vc_ref[...] = _dot(ckvn, wv_ref[...]).astype(BF16)


def _cd_proj(x, g, w, qn, kvn, wq1, wq2, wk, wv, cos_t, sin_t, t):
    n, d = x.shape
    tm = min(ROW_TILE, t)
    nt = t // tm
    full = _const_spec
    consts = [g.reshape(1, d), w, qn.reshape(1, -1), kvn.reshape(1, -1), wq1, wq2, wk, wv]
    out_w = (C_HEADS * LANES, C_HEADS * LANES, C_HEADS * C_V) + CD_PROJ_WIDTHS[4:]
    return pl.pallas_call(
        _cd_proj_kernel,
        grid=(n // tm,),
        in_specs=[pl.BlockSpec((tm, d), lambda i: (i, 0))] + [full(a) for a in consts]
        + [pl.BlockSpec((tm, LANES), lambda i: (i % nt, 0)),
           pl.BlockSpec((tm, LANES), lambda i: (i % nt, 0))],
        out_specs=[pl.BlockSpec((tm, wd), lambda i: (i, 0)) for wd in out_w],
        out_shape=[jax.ShapeDtypeStruct((n, wd), BF16) for wd in out_w],
        compiler_params=_params(("parallel",)),
        name="cd_proj",
    )(x, *consts, cos_t, sin_t)


def _pair_output(o, tq, hd):
    return jnp.concatenate([o[:hd, :tq], o[hd:, tq:]], axis=0).T


def _attn_c_kernel(q_ref, k_ref, v_ref, o_ref, vt_ref, *refs, t, tq, tk):
    acc_ref, l_ref = refs[2], refs[1]
    @pl.when(pl.program_id(2) == 0)
    def _():
        _fill_vt(v_ref, vt_ref, tk)

    q0 = q_ref[:, :LANES]
    q1 = q_ref[:, LANES:]
    nk = t // tk

    def scores(j):
        rows = slice(j * tk, (j + 1) * tk)
        return jnp.concatenate([_dot_nt(k_ref[rows, :LANES], q0), _dot_nt(k_ref[rows, LANES:], q1)],
                               axis=1)

    _attention_chunks(refs, 0, nk, scores, lambda j: vt_ref[j])
    o_ref[...] = _pair_output(acc_ref[...] / l_ref[...], tq, C_V).astype(o_ref.dtype)


def _attn_c(qc, kc, vc, batch, t):
    tq = min(256, t)
    tk = min(1024, t)
    nq = t // tq
    pairs = C_HEADS // 2
    return pl.pallas_call(
        functools.partial(_attn_c_kernel, t=t, tq=tq, tk=tk),
        grid=(batch, pairs, nq),
        in_specs=[pl.BlockSpec((tq, 2 * LANES), lambda b, p, i: (b * nq + i, p)),
                  pl.BlockSpec((t, 2 * LANES), lambda b, p, i: (b, p)),
                  pl.BlockSpec((t, LANES), lambda b, p, i: (b, p))],
        out_specs=pl.BlockSpec((tq, LANES), lambda b, p, i: (b * nq + i, p)),
        out_shape=jax.ShapeDtypeStruct(vc.shape, BF16),
        scratch_shapes=_attn_scratch(2 * tq, LANES, t, tk),
        compiler_params=_params(("parallel", "parallel", "arbitrary")),
        name="attn_c",
    )(qc, kc, vc)


D_REACH = max((w // (2 * dil)) * dil for w, dil in D_PATTERNS)


def _d_tile_range(tq, tk):
    return -((D_REACH + tk) // tq) + 1, (tq + D_REACH) // tq - 1


def _attn_d_kernel(slope_ref, q_ref, k_ref, v_ref, o_ref, vt_ref, *rest, t, tq, tk):
    refs, bm_ref = rest[:-1], rest[-1]
    acc_ref, l_ref = refs[2], refs[1]
    p = pl.program_id(0)
    b = pl.program_id(1)
    i = pl.program_id(2)
    nk = t // tk
    u_lo, u_hi = _d_tile_range(tq, tk)
    masked_tile = u_hi - u_lo + 1

    @pl.when((b == 0) & (i == 0))
    def _():
        s0 = slope_ref[2 * p] * LOG2E
        s1 = slope_ref[2 * p + 1] * LOG2E
        amb = (lax.broadcasted_iota(jnp.int32, (tk, tq), 0)
               - lax.broadcasted_iota(jnp.int32, (tk, tq), 1))
        for u in range(u_lo, u_hi + 1):
            d = amb + u * tq
            ad = jnp.abs(d)
            mult = jnp.zeros((tk, tq), F32)
            for w, dil in D_PATTERNS:
                hit = (ad <= (w // (2 * dil)) * dil) & ((d & (dil - 1)) == 0)
                mult = mult + jnp.where(hit, 1.0, 0.0)
            adf = ad.astype(F32)
            logm = jnp.log2(jnp.maximum(mult, 1.0))
            seen = mult > 0.0
            bm_ref[u - u_lo] = jnp.concatenate([jnp.where(seen, s0 * adf - logm, MASKED),
                                                jnp.where(seen, s1 * adf - logm, MASKED)], axis=1)
        bm_ref[masked_tile] = jnp.full((tk, 2 * tq), MASKED, F32)

    @pl.when(i == 0)
    def _():
        _fill_vt(v_ref, vt_ref, tk)

    q = q_ref[...]
    lane = lax.broadcasted_iota(jnp.int32, (tq, LANES), 1)
    low = lane < HEAD_DIM
    zero = jnp.zeros_like(q)
    qq = jnp.concatenate([jnp.where(low, q, zero), jnp.where(low, zero, q)], axis=0)
    nwin = min(nk, (tq + 2 * D_REACH + tk - 1) // tk + 1)
    c_first = jnp.clip((i * tq - D_REACH) // tk, 0, nk - nwin)

    def scores(c):
        u = c * (tk // tq) - i
        tile = jnp.where((u >= u_lo) & (u <= u_hi), u - u_lo, masked_tile)
        return _dot_nt(k_ref[pl.ds(pl.multiple_of(c * tk, tk), tk), :], qq) - bm_ref[tile]

    _attention_chunks(refs, c_first, nwin, scores, lambda c: vt_ref[c])
    o_ref[...] = _pair_output(acc_ref[...] / l_ref[...], tq, HEAD_DIM).astype(o_ref.dtype)


def _attn_d(qd, kd, vd, batch, t):
    tq = min(256, t)
    tk = min(512, t)
    nq = t // tq
    pairs = D_HEADS // 2
    slopes = jnp.asarray(_alibi_slopes(D_HEADS), F32)
    u_lo, u_hi = _d_tile_range(tq, tk)
    return pl.pallas_call(
        functools.partial(_attn_d_kernel, t=t, tq=tq, tk=tk),
        grid=(pairs, batch, nq),
        in_specs=[pl.BlockSpec(memory_space=pltpu.SMEM),
                  pl.BlockSpec((tq, LANES), lambda p, b, i: (b * nq + i, p)),
                  pl.BlockSpec((t, LANES), lambda p, b, i: (b, p)),
                  pl.BlockSpec((t, LANES), lambda p, b, i: (b, p))],
        out_specs=pl.BlockSpec((tq, LANES), lambda p, b, i: (b * nq + i, p)),
        out_shape=jax.ShapeDtypeStruct(qd.shape, BF16),
        scratch_shapes=_attn_scratch(2 * tq, LANES, t, tk)
        + [pltpu.VMEM((u_hi - u_lo + 2, tk, 2 * tq), F32)],
        compiler_params=_params(("arbitrary", "arbitrary", "arbitrary")),
        name="attn_d",
    )(slopes, qd, kd, vd)


def _router_kernel(x_ref, g_ref, rt_ref, xn_ref, ids_ref, gates_ref):
    xn = _rms(x_ref[...], g_ref[...])
    xn_ref[...] = xn
    x_hi = xn.astype(BF16)
    x_lo = (xn - x_hi.astype(F32)).astype(BF16)
    rt = rt_ref[...]
    r_hi = rt.astype(BF16)
    r_lo = (rt - r_hi.astype(F32)).astype(BF16)
    logits = _dot_nt(r_hi, x_hi) + _dot_nt(r_hi, x_lo) + _dot_nt(r_lo, x_hi)
    eid = lax.broadcasted_iota(jnp.int32, logits.shape, 0)
    v1 = jnp.max(logits, axis=0, keepdims=True)
    i1 = jnp.min(jnp.where(logits == v1, eid, N_EXPERTS), axis=0, keepdims=True)
    rest = jnp.where(eid == i1, -jnp.inf, logits)
    v2 = jnp.max(rest, axis=0, keepdims=True)
    i2 = jnp.min(jnp.where(rest == v2, eid, N_EXPERTS), axis=0, keepdims=True)
    e2 = jnp.exp(v2 - v1)
    g1 = 1.0 / (1.0 + e2)
    ids_ref[...] = jnp.concatenate([i1, i2], axis=0)
    gates_ref[...] = jnp.concatenate([g1, e2 * g1], axis=0)


def _router(x, g, router_t):
    n, d = x.shape
    tm = min(ROW_TILE, n)
    return pl.pallas_call(
        _router_kernel,
        grid=(n // tm,),
        in_specs=[pl.BlockSpec((tm, d), lambda i: (i, 0)),
                  pl.BlockSpec((1, d), lambda i: (0, 0)),
                  pl.BlockSpec(router_t.shape, lambda i: (0, 0))],
        out_specs=[pl.BlockSpec((tm, d), lambda i: (i, 0)),
                   pl.BlockSpec((2, tm), lambda i: (0, i)),
                   pl.BlockSpec((2, tm), lambda i: (0, i))],
        out_shape=[jax.ShapeDtypeStruct((n, d), F32),
                   jax.ShapeDtypeStruct((2, n), jnp.int32),
                   jax.ShapeDtypeStruct((2, n), F32)],
        compiler_params=_params(("parallel",)),
        name="router",
    )(x, g.reshape(1, d), router_t)


def _moe_kernel(te_ref, nv_ref, idx_hbm, x_hbm, wg_ref, wu_ref, wd_ref, o_ref,
                idx_ref, xbuf_ref, xb_ref, acc_ref, isem, rsem, *, nf, per_step):
    ti = pl.program_id(0)
    f = pl.program_id(1)
    nv = nv_ref[0]
    live = ti < nv
    tm = xb_ref.shape[0]

    def idx_copy(tile, slot):
        return pltpu.make_async_copy(idx_hbm.at[pl.ds(pl.multiple_of(tile * tm, tm), tm)],
                                     idx_ref.at[slot], isem.at[slot])

    def row_copy(tok, r, slot):
        return pltpu.make_async_copy(x_hbm.at[pl.ds(tok, 1)], xbuf_ref.at[slot, pl.ds(r, 1)],
                                     rsem.at[slot])

    def wait_rows(slot):
        def row(r, carry):
            row_copy(0, 0, slot).wait()
            return carry
        lax.fori_loop(0, nf * per_step, row, 0, unroll=8)

    @pl.when((ti == 0) & (f == 0))
    def _():
        idx_copy(0, 0).start()
        idx_copy(0, 0).wait()

        def row(r, carry):
            row_copy(idx_ref[0, jnp.minimum(r, tm - 1)], r, 0).start()
            return carry
        lax.fori_loop(0, nf * per_step, row, 0, unroll=8)

        @pl.when(1 < nv)
        def _():
            idx_copy(1, 1).start()

    for slot in range(2):
        @pl.when(live & (f == 0) & (ti % 2 == slot))
        def _(slot=slot):
            wait_rows(slot)

            @pl.when(ti + 1 < nv)
            def _():
                idx_copy(ti + 1, 1 - slot).wait()

            @pl.when(ti + 2 < nv)
            def _():
                idx_copy(ti + 2, slot).start()

            xb_ref[...] = xbuf_ref[slot, :tm, :].astype(BF16)
            acc_ref[...] = jnp.zeros(acc_ref.shape, F32)

    @pl.when(live)
    def _():
        nxt = jnp.minimum(ti + 1, nv - 1)
        islot = nxt % 2
        nslot = 1 - ti % 2
        for k in range(per_step):
            r = f * per_step + k
            rr = jnp.minimum(r, tm - 1)
            row_copy(idx_ref[islot, rr], r, nslot).start()
        xb = xb_ref[...]
        hg = _dot(xb, wg_ref[...])
        hu = _dot(xb, wu_ref[...])
        act = (hg / (1.0 + jnp.exp(-hg)) * hu).astype(BF16)
        acc_ref[...] += _dot(act, wd_ref[...])

    for slot in range(2):
        @pl.when((ti == nv - 1) & (f == nf - 1) & (ti % 2 == slot))
        def _(slot=slot):
            wait_rows(1 - slot)

    @pl.when(f == nf - 1)
    def _():
        o_ref[...] = jnp.where(live, acc_ref[...], 0.0)


def _moe(tile_expert, n_valid, src_tok, xn, wg, wu, wd):
    p = src_tok.shape[0]
    d = xn.shape[1]
    tm = MOE_TILE
    fc = MOE_FCHUNK
    nf = wg.shape[2] // fc
    per_step = -(-tm // nf)

    def tile(ti, nv):
        return jnp.minimum(ti, nv[0] - 1)

    grid_spec = pltpu.PrefetchScalarGridSpec(
        num_scalar_prefetch=2,
        grid=(p // tm, nf),
        in_specs=[pl.BlockSpec(memory_space=pl.ANY), pl.BlockSpec(memory_space=pl.ANY),
                  pl.BlockSpec((None, d, fc), lambda ti, f, te, nv: (te[tile(ti, nv)], 0, jnp.where(ti < nv[0], f, nf - 1))),
                  pl.BlockSpec((None, d, fc), lambda ti, f, te, nv: (te[tile(ti, nv)], 0, jnp.where(ti < nv[0], f, nf - 1))),
                  pl.BlockSpec((None, fc, d), lambda ti, f, te, nv: (te[tile(ti, nv)], jnp.where(ti < nv[0], f, nf - 1), 0))],
        out_specs=pl.BlockSpec((tm, d), lambda ti, f, te, nv: (ti, 0)),
        scratch_shapes=[pltpu.SMEM((2, tm), jnp.int32),
                        pltpu.VMEM((2, -(-nf * per_step // 8) * 8, d), F32), pltpu.VMEM((tm, d), BF16), pltpu.VMEM((tm, d), F32),
                        pltpu.SemaphoreType.DMA((2,)), pltpu.SemaphoreType.DMA((2,))],
    )
    return pl.pallas_call(
        functools.partial(_moe_kernel, nf=nf, per_step=per_step),
        grid_spec=grid_spec,
        out_shape=jax.ShapeDtypeStruct((p, d), F32),
        compiler_params=_params(("arbitrary", "arbitrary")),
        name="moe_experts",
    )(tile_expert, n_valid, src_tok, xn, wg, wu, wd)


def _combine_kernel(pos_hbm, x_ref, gate_ref, g_ref, ys_hbm, o_ref, idx_a, idx_b, buf_ref, isem, rsem,
                    *, rt, step0, nsteps):
    s = pl.program_id(0)
    idx_slots = (idx_a, idx_b)

    def idx_copy(step, slot):
        start = pl.multiple_of((step + step0) * 2 * rt, 2 * rt)
        return pltpu.make_async_copy(pos_hbm.at[pl.ds(start, 2 * rt)], idx_slots[slot], isem.at[slot])

    def row_copy(src, k, r, slot):
        return pltpu.make_async_copy(ys_hbm.at[pl.ds(src, 1)], buf_ref.at[slot, k, pl.ds(r, 1)],
                                     rsem.at[slot])

    def issue_rows(slot):
        def row(j, carry):
            for k in range(2):
                row_copy(idx_slots[slot][k * rt + j], k, j, slot).start()
            return carry
        lax.fori_loop(0, rt, row, 0, unroll=8)

    def wait_rows(slot):
        def row(j, carry):
            for k in range(2):
                row_copy(0, k, 0, slot).wait()
            return carry
        lax.fori_loop(0, rt, row, 0, unroll=8)

    @pl.when(s == 0)
    def _():
        idx_copy(0, 0).start()
        idx_copy(0, 0).wait()
        issue_rows(0)
        if nsteps > 1:
            idx_copy(1, 1).start()

    def step(slot):
        @pl.when(s + 1 < nsteps)
        def _():
            idx_copy(s + 1, 1 - slot).wait()
            issue_rows(1 - slot)

        @pl.when(s + 2 < nsteps)
        def _():
            idx_copy(s + 2, slot).start()

        wait_rows(slot)
        gates = gate_ref[...]
        y = x_ref[...] + gates[:, 0:1] * buf_ref[slot, 0] + gates[:, 1:2] * buf_ref[slot, 1]
        o_ref[...] = _rms(y, g_ref[...])

    for slot in range(2):
        @pl.when(s % 2 == slot)
        def _(slot=slot):
            step(slot)


def _combine(pos_flat, x, gates_t, g, ys, row0, nrows):
    d = x.shape[1]
    rt = GATHER_CHUNK // 2
    step0 = row0 // rt
    nsteps = nrows // rt
    return pl.pallas_call(
        functools.partial(_combine_kernel, rt=rt, step0=step0, nsteps=nsteps),
        grid=(nsteps,),
        in_specs=[pl.BlockSpec(memory_space=pl.ANY),
                  pl.BlockSpec((rt, d), lambda i: (i + step0, 0)),
                  pl.BlockSpec((rt, 2), lambda i: (i + step0, 0)),
                  pl.BlockSpec((1, d), lambda i: (0, 0)),
                  pl.BlockSpec(memory_space=pl.ANY)],
        out_specs=pl.BlockSpec((rt, d), lambda i: (i, 0)),
        out_shape=jax.ShapeDtypeStruct((nrows, d), F32),
        scratch_shapes=[pltpu.SMEM((2 * rt,), jnp.int32), pltpu.SMEM((2 * rt,), jnp.int32),
                        pltpu.VMEM((2, 2, rt, d), F32),
                        pltpu.SemaphoreType.DMA((2,)), pltpu.SemaphoreType.DMA((2,))],
        compiler_params=_params(("arbitrary",)),
        name="combine",
    )(pos_flat, x, gates_t, g.reshape(1, d), ys)


def _pad_cols(w, lo, total):
    return jnp.pad(w, ((0, 0), (lo, total - lo - w.shape[1])))


def _rot_half_cols(w):
    half = w.shape[1] // 2
    return jnp.concatenate([-w[:, half:], w[:, :half]], axis=1)


def _a_head_perm():
    half = A_HEADS // 2
    cols = []
    for p in range(half):
        cols += list(range(HEAD_DIM * p, HEAD_DIM * (p + 1)))
        cols += list(range(HEAD_DIM * (p + half), HEAD_DIM * (p + half + 1)))
    return np.asarray(cols, np.int32)


def _rope_tables(t):
    half = C_ROPE // 2
    inv = (ROPE_THETA ** (-np.arange(half) * 2.0 / C_ROPE)).astype(np.float32).astype(np.float64)
    ang = np.arange(t, dtype=np.float64)[:, None] * inv[None, :]
    cos = np.concatenate([np.cos(ang), np.cos(ang)], axis=1)
    sin = np.concatenate([np.sin(ang), np.sin(ang)], axis=1)
    pad = LANES - C_NOPE - C_ROPE
    cos_t = np.concatenate([np.ones((t, C_NOPE)), cos, np.zeros((t, pad))], axis=1)
    sin_t = np.concatenate([np.zeros((t, C_NOPE)), sin, np.zeros((t, pad))], axis=1)
    return jnp.asarray(cos_t, F32), jnp.asarray(sin_t, F32)


def _layer_ab(x, batch, t, norm_attn, w_in, sink, lq1, lk1, lq2, lk2, subln, w_out, norm_ffn,
              w_gate, w_up, w_down, lam_init):
    perm = _a_head_perm()
    qa_w = A_HEADS * HEAD_DIM
    w_in_k = jnp.concatenate([w_in[:, :qa_w][:, perm], w_in[:, qa_w:]], axis=1).astype(BF16)
    scale = HEAD_DIM ** -0.5
    qa, ka, va, qd, kd, vd = _norm_proj(x, norm_attn, w_in_k, AB_WIDTHS,
                                        (scale * LOG2E, 1.0, 1.0, scale * LOG2E, 1.0, 1.0))
    oa = _attn_a(sink, qa, ka, va, batch, t)
    od = _attn_b(lq1, lk1, lq2, lk2, subln, qd, kd, vd, batch, t, lam_init)
    x = _out_proj(x, oa, od, w_out[:qa_w][perm].astype(BF16), w_out[qa_w:].astype(BF16))
    nchunks = w_gate.shape[1] // FFN_CHUNK
    to_chunks = lambda w: w.reshape(w.shape[0], nchunks, FFN_CHUNK).transpose(1, 0, 2).astype(BF16)
    return _ffn(x, norm_ffn, to_chunks(w_gate), to_chunks(w_up), w_down.astype(BF16))


def _cd_weights(w_in, w_uq, w_ukv):
    o = np.cumsum((0,) + CD_WIDTHS)
    w_kr = w_in[:, o[2]:o[3]]
    w_proj = jnp.concatenate([w_in[:, :o[2]], _pad_cols(w_kr, C_NOPE, LANES),
                              _pad_cols(_rot_half_cols(w_kr), C_NOPE, LANES), w_in[:, o[3]:]], axis=1)
    dq = C_NOPE + C_ROPE
    wq1, wq2, wk, wv = [], [], [], []
    for h in range(C_HEADS):
        wq_h = w_uq[:, dq * h:dq * (h + 1)]
        wq1.append(_pad_cols(wq_h, 0, LANES))
        wq2.append(_pad_cols(_rot_half_cols(wq_h[:, C_NOPE:]), C_NOPE, LANES))
        wkv_h = w_ukv[:, (C_NOPE + C_V) * h:(C_NOPE + C_V) * (h + 1)]
        wk.append(_pad_cols(wkv_h[:, :C_NOPE], 0, LANES))
        wv.append(wkv_h[:, C_NOPE:])
    cat = lambda ws: jnp.concatenate(ws, axis=1).astype(BF16)
    return w_proj.astype(BF16), cat(wq1), cat(wq2), cat(wk), cat(wv)


def _route(ids, n):
    e = ids.reshape(-1)
    onehot = (e[:, None] == jnp.arange(N_EXPERTS, dtype=jnp.int32)[None, :]).astype(jnp.int32)
    csum = jnp.cumsum(onehot, axis=0)
    rank = jnp.sum(onehot * (csum - 1), axis=1)
    counts = csum[-1]
    padded = ((counts + MOE_TILE - 1) // MOE_TILE) * MOE_TILE
    gend = jnp.cumsum(padded)
    gstart = gend - padded
    slot = gstart[e] + rank
    p_total = 2 * n + N_EXPERTS * MOE_TILE
    tok = jnp.tile(jnp.arange(n, dtype=jnp.int32), 2)
    src_tok = jnp.zeros((p_total,), jnp.int32).at[slot].set(tok)
    ntiles = p_total // MOE_TILE
    tile_start = jnp.arange(ntiles, dtype=jnp.int32) * MOE_TILE
    tile_expert = jnp.minimum(jnp.sum((tile_start[:, None] >= gend[None, :]).astype(jnp.int32), axis=1),
                              N_EXPERTS - 1).astype(jnp.int32)
    n_valid = (gend[-1] // MOE_TILE).astype(jnp.int32).reshape(1)
    return slot.astype(jnp.int32), src_tok, tile_expert, n_valid


def _layer_cd(x, batch, t, norm_attn, w_in, q_norm, kv_norm, w_uq, w_ukv, w_out, norm_ffn, router,
              w_gate, w_up, w_down, final_norm, splits):
    n = x.shape[0]
    w_proj, wq1, wq2, wk, wv = _cd_weights(w_in, w_uq, w_ukv)
    cos_t, sin_t = _rope_tables(t)
    qc, kc, vc, qd, kd, vd = _cd_proj(x, norm_attn, w_proj, q_norm, kv_norm, wq1, wq2, wk, wv,
                                      cos_t, sin_t, t)
    oc = _attn_c(qc, kc, vc, batch, t)
    od = _attn_d(qd, kd, vd, batch, t)
    cw = C_HEADS * C_V
    x = _out_proj(x, oc, od, w_out[:cw].astype(BF16), w_out[cw:].astype(BF16))
    xn, ids, gates = _router(x, norm_ffn, router.T)
    slot, src_tok, tile_expert, n_valid = _route(ids, n)
    ys = _moe(tile_expert, n_valid, src_tok, xn, w_gate.astype(BF16), w_up.astype(BF16),
              w_down.astype(BF16))
    rt = GATHER_CHUNK // 2
    pos_flat = slot.reshape(2, n // rt, rt).transpose(1, 0, 2).reshape(-1)
    gates_t = gates.T
    outs = []
    row0 = 0
    for rows in splits:
        outs.append(_combine(pos_flat, x, gates_t, final_norm, ys, row0, rows))
        row0 += rows
    return outs


def kernel(x_prompt, x_sample, ab_norm_attn, ab_w_in, ab_sink, ab_lambda_q1, ab_lambda_k1, ab_lambda_q2, ab_lambda_k2, ab_subln, ab_w_out, ab_norm_ffn, ffn_w_gate, ffn_w_up, ffn_w_down, cd_norm_attn, cd_w_in, cd_q_norm, cd_kv_norm, cd_w_uq, cd_w_ukv, cd_w_out, cd_norm_ffn, moe_router, moe_w_gate, moe_w_up, moe_w_down, final_norm):
    bp, t, d = x_prompt.shape
    bs = x_sample.shape[0]
    assert x_sample.shape[1] == t
    batch = bp + bs
    x = jnp.concatenate([x_prompt.reshape(bp * t, d), x_sample.reshape(bs * t, d)], axis=0)
    lam_init = 0.8 - 0.6 * math.exp(-0.3 * 0)
    x = _layer_ab(x, batch, t, ab_norm_attn[0], ab_w_in[0], ab_sink[0], ab_lambda_q1[0],
                  ab_lambda_k1[0], ab_lambda_q2[0], ab_lambda_k2[0], ab_subln[0], ab_w_out[0],
                  ab_norm_ffn[0], ffn_w_gate[0], ffn_w_up[0], ffn_w_down[0], lam_init)
    y_prompt, y_sample = _layer_cd(x, batch, t, cd_norm_attn[0], cd_w_in[0], cd_q_norm[0],
                                   cd_kv_norm[0], cd_w_uq[0], cd_w_ukv[0], cd_w_out[0],
                                   cd_norm_ffn[0], moe_router[0], moe_w_gate[0], moe_w_up[0],
                                   moe_w_down[0], final_norm, (bp * t, bs * t))
    return (y_prompt.reshape(bp, t, d), y_sample.reshape(bs, t, d))
```

```python
import functools
import math

import numpy as np
import jax
import jax.numpy as jnp
from jax import lax
from jax.experimental import pallas as pl
from jax.experimental.pallas import tpu as pltpu

F32 = jnp.float32
BF16 = jnp.bfloat16

D_MODEL = 1024
HEAD_DIM = 64
A_HEADS = 8
A_KV_HEADS = 2
A_WINDOW = 128
A_BLOCK = 128
B_HEADS = 4
B_V_DIM = 2 * HEAD_DIM
C_HEADS = 8
C_Q_RANK = 384
C_KV_RANK = 256
C_NOPE = 64
C_ROPE = 32
C_V = 64
D_HEADS = 8
D_PATTERNS = ((128, 1), (512, 4), (2048, 16))
FFN_DIM = 2816
N_EXPERTS = 8
EXPERT_DIM = 3584
ROPE_THETA = 10000.0
RMS_EPS = 1e-6
NEG_INF = -1e30
LOG2E = math.log2(math.e)
LANES = 128
VMEM_LIMIT = 56 * 1024 * 1024

AB_WIDTHS = (A_HEADS * HEAD_DIM, A_KV_HEADS * HEAD_DIM, A_KV_HEADS * HEAD_DIM,
             B_HEADS * 2 * HEAD_DIM, B_HEADS * 2 * HEAD_DIM, B_HEADS * B_V_DIM)
CD_WIDTHS = (C_Q_RANK, C_KV_RANK, C_ROPE,
             D_HEADS * HEAD_DIM, D_HEADS * HEAD_DIM, D_HEADS * HEAD_DIM)
CD_PROJ_WIDTHS = (C_Q_RANK, C_KV_RANK, LANES, LANES,
                  D_HEADS * HEAD_DIM, D_HEADS * HEAD_DIM, D_HEADS * HEAD_DIM)

ROW_TILE = 512
MOE_TILE = 1024
MOE_FCHUNK = 512
GATHER_CHUNK = 1024
FFN_CHUNK = 256


def _alibi_slopes(n):
    return [float(2.0 ** (-8.0 * (i + 1) / n)) for i in range(n)]


def _params(sem, vmem=VMEM_LIMIT):
    return pltpu.CompilerParams(dimension_semantics=sem, vmem_limit_bytes=vmem)


def _const_spec(a):
    return pl.BlockSpec(a.shape, lambda *_: (0,) * a.ndim, pipeline_mode=pl.Buffered(1))


def _rms(x, g):
    ms = jnp.mean(x * x, axis=-1, keepdims=True)
    return x * lax.rsqrt(ms + RMS_EPS) * g


def _dot(a, b):
    return jnp.dot(a, b, preferred_element_type=F32)


def _dot_nt(a, b):
    return lax.dot_general(a, b, (((1,), (1,)), ((), ())), preferred_element_type=F32)


def _norm_proj_kernel(x_ref, g_ref, w_ref, *o_refs, widths, scales):
    xn = _rms(x_ref[...], g_ref[...]).astype(BF16)
    off = 0
    for o_ref, w, sc in zip(o_refs, widths, scales):
        y = _dot(xn, w_ref[:, off:off + w])
        if sc != 1.0:
            y = y * sc
        o_ref[...] = y.astype(o_ref.dtype)
        off += w


def _norm_proj(x, g, w, widths, scales):
    n, d = x.shape
    tm = min(ROW_TILE, n)
    return pl.pallas_call(
        functools.partial(_norm_proj_kernel, widths=widths, scales=scales),
        grid=(n // tm,),
        in_specs=[pl.BlockSpec((tm, d), lambda i: (i, 0)),
                  pl.BlockSpec((1, d), lambda i: (0, 0)),
                  pl.BlockSpec(w.shape, lambda i: (0, 0))],
        out_specs=[pl.BlockSpec((tm, wd), lambda i: (i, 0)) for wd in widths],
        out_shape=[jax.ShapeDtypeStruct((n, wd), BF16) for wd in widths],
        compiler_params=_params(("parallel",)),
        name="norm_proj",
    )(x, g.reshape(1, d), w)


def _attn_a_kernel(sink_ref, q_ref, kp_ref, kc_ref, kn_ref, vp_ref, vc_ref, vn_ref, o_ref, *, nb):
    i = pl.program_id(1)
    blk = A_BLOCK
    half_heads = A_HEADS // 2
    kcat = jnp.concatenate([kp_ref[...], kc_ref[...], kn_ref[...]], axis=0)
    vcat = jnp.concatenate([vp_ref[...], vc_ref[...], vn_ref[...]], axis=0)
    vt = vcat.astype(F32).T.astype(BF16)
    lane = lax.broadcasted_iota(jnp.int32, (blk, LANES), 1)
    low = lane < HEAD_DIM
    parts = []
    for half in range(2):
        for p in range(half_heads):
            qp = q_ref[:, LANES * p:LANES * (p + 1)]
            parts.append(jnp.where(low if half == 0 else jnp.logical_not(low), qp, jnp.zeros_like(qp)))
    st = _dot_nt(kcat, jnp.concatenate(parts, axis=0))
    key = lax.broadcasted_iota(jnp.int32, (3 * blk, blk), 0)
    qry = lax.broadcasted_iota(jnp.int32, (3 * blk, blk), 1)
    dist = jnp.abs(key - blk - qry)
    valid = (dist <= A_WINDOW) & ((key >= blk) | (i > 0)) & ((key < 2 * blk) | (i < nb - 1))
    distf = dist.astype(F32)
    slopes = _alibi_slopes(A_HEADS)
    probs, scales = [], []
    for h in range(A_HEADS):
        sh = jnp.where(valid, st[:, blk * h:blk * (h + 1)] - (slopes[h] * LOG2E) * distf, NEG_INF)
        m = jnp.max(sh, axis=0, keepdims=True)
        e = jnp.exp2(sh - m)
        l = jnp.sum(e, axis=0, keepdims=True)
        lse = m * (1.0 / LOG2E) + jnp.log(l)
        gate = 1.0 / (1.0 + jnp.exp(sink_ref[h] - lse))
        probs.append(e.astype(BF16))
        scales.append(gate / l)
    o = _dot(vt, jnp.concatenate(probs, axis=1)) * jnp.concatenate(scales, axis=1)
    for p in range(half_heads):
        top = o[:HEAD_DIM, blk * p:blk * (p + 1)]
        bot = o[HEAD_DIM:, blk * (p + half_heads):blk * (p + half_heads + 1)]
        o_ref[:, LANES * p:LANES * (p + 1)] = jnp.concatenate([top, bot], axis=0).T.astype(o_ref.dtype)


def _attn_a(sink, qa, ka, va, batch, t):
    nb = t // A_BLOCK
    qw = qa.shape[1]
    kw = ka.shape[1]

    def nbr(delta):
        return lambda b, i: (b * nb + jnp.clip(i + delta, 0, nb - 1), 0)

    kv_specs = [pl.BlockSpec((A_BLOCK, kw), nbr(d)) for d in (-1, 0, 1)]
    return pl.pallas_call(
        functools.partial(_attn_a_kernel, nb=nb),
        grid=(batch, nb),
        in_specs=[pl.BlockSpec(memory_space=pltpu.SMEM),
                  pl.BlockSpec((A_BLOCK, qw), lambda b, i: (b * nb + i, 0))] + kv_specs + kv_specs,
        out_specs=pl.BlockSpec((A_BLOCK, qw), lambda b, i: (b * nb + i, 0)),
        out_shape=jax.ShapeDtypeStruct(qa.shape, BF16),
        compiler_params=_params(("parallel", "parallel")),
        name="attn_a",
    )(sink, qa, ka, ka, ka, va, va, va)


def _softmax_step_t(st, vt, m_ref, acc_ref, offset=None, l_ref=None):
    m_old = m_ref[...]
    smax = jnp.max(st, axis=0, keepdims=True)
    if offset is None:
        m_new = jnp.maximum(m_old, smax)
        shift = m_new
    else:
        m_new = jnp.maximum(m_old, smax - offset)
        shift = m_new + offset
    alpha = jnp.exp2(m_old - m_new)
    if l_ref is None:
        p = jnp.exp2((st - shift).astype(BF16))
        acc_ref[...] = alpha * acc_ref[...] + _dot(vt, p)
    else:
        p = jnp.exp2(st - shift)
        l_ref[...] = alpha * l_ref[...] + jnp.sum(p, axis=0, keepdims=True)
        vdim = vt.shape[0]
        acc_ref[:vdim, :] = alpha * acc_ref[:vdim, :] + _dot(vt, p.astype(BF16))
    m_ref[...] = m_new


def _attention_chunks(refs, first, n, scores, vt_at, offset=None, l_ref=None):
    m_ref, acc_ref, sa_ref, sb_ref = refs
    bufs = (sa_ref, sb_ref)
    m_ref[...] = jnp.full(m_ref.shape, M_INIT, F32)
    acc_ref[...] = jnp.zeros(acc_ref.shape, F32)
    if l_ref is not None:
        l_ref[...] = jnp.zeros(l_ref.shape, F32)
    sa_ref[...] = scores(first)
    for k in range(n):
        if k + 1 < n:
            bufs[(k + 1) % 2][...] = scores(first + k + 1)
        _softmax_step_t(bufs[k % 2][...], vt_at(first + k), m_ref, acc_ref,
                        None if offset is None else offset(first + k), l_ref)


MASKED = 1e30
M_INIT = -1e29


ONES_ROWS = 16


def _normalised(acc_ref, l_ref=None):
    vdim = acc_ref.shape[0] - ONES_ROWS
    return acc_ref[:vdim, :] / (acc_ref[vdim:vdim + 1, :] if l_ref is None else l_ref[...])


def _fill_vt(v_ref, vt_ref, tk, group):
    stride = group + ONES_ROWS
    ones = jnp.ones((ONES_ROWS, tk), vt_ref.dtype)
    for c in range(vt_ref.shape[0]):
        vt = v_ref[c * tk:(c + 1) * tk, :].astype(F32).T.astype(vt_ref.dtype)
        for g in range(v_ref.shape[1] // group):
            vt_ref[c, stride * g:stride * g + group, :] = vt[group * g:group * (g + 1), :]
            vt_ref[c, stride * g + group:stride * (g + 1), :] = ones


def _attn_scratch(lanes, vdim, t, tk):
    groups = vdim // LANES
    return [pltpu.VMEM((t // tk, groups * (LANES + ONES_ROWS), tk), BF16),
            pltpu.VMEM((1, lanes), F32), pltpu.VMEM((LANES + ONES_ROWS, lanes), F32),
            pltpu.VMEM((tk, lanes), F32), pltpu.VMEM((tk, lanes), F32)]


def _attn_b_kernel(lq1_ref, lk1_ref, lq2_ref, lk2_ref, subln_ref, q_ref, k_ref, v_ref, o_ref,
                   vt_ref, *rest, lam_init, t, tq, tk):
    refs, bias_ref = rest[:-1], rest[-1]
    acc_ref = refs[1]
    i = pl.program_id(1)
    r = tk // tq
    nk = t // tk
    slopes = [s * LOG2E for s in _alibi_slopes(B_HEADS)]

    @pl.when(i == 0)
    def _():
        _fill_vt(v_ref, vt_ref, tk, LANES)
        amb = (lax.broadcasted_iota(jnp.int32, (tk, tq), 0)
               - lax.broadcasted_iota(jnp.int32, (tk, tq), 1))
        for h in range(B_HEADS):
            lin = slopes[h] * amb.astype(F32)
            bias_ref[h, 0] = lin
            bias_ref[h, 1] = -lin
            for s in range(r):
                bias_ref[h, 2 + s] = slopes[h] * jnp.abs(amb - s * tq).astype(F32)

    lam = (jnp.exp(jnp.sum(lq1_ref[...] * lk1_ref[...], axis=-1, keepdims=True))
           - jnp.exp(jnp.sum(lq2_ref[...] * lk2_ref[...], axis=-1, keepdims=True)) + lam_init)
    lane = lax.broadcasted_iota(jnp.int32, (tq, LANES), 1)
    low = lane < HEAD_DIM
    jd = i // r
    for h in range(B_HEADS):
        cols = slice(LANES * h, LANES * (h + 1))
        qh = q_ref[:, cols]
        zero = jnp.zeros_like(qh)
        qq = jnp.concatenate([jnp.where(low, qh, zero), jnp.where(low, zero, qh)], axis=0)

        def scores(j, h=h, cols=cols, qq=qq):
            kj = k_ref[j * tk:(j + 1) * tk, cols]
            sel = jnp.where(j > jd, 0, jnp.where(j < jd, 1, 2 + i % r))
            bias = bias_ref[h, sel]
            return _dot_nt(kj, qq) - jnp.concatenate([bias, bias], axis=1)

        def offset(j, slope=slopes[h]):
            sign = jnp.where(j > jd, 1.0, jnp.where(j < jd, -1.0, 0.0))
            return sign * slope * (j * tk - i * tq).astype(F32)

        rows = slice((LANES + ONES_ROWS) * h, (LANES + ONES_ROWS) * (h + 1))
        _attention_chunks(refs, 0, nk, scores, lambda j, rows=rows: vt_ref[j, rows, :], offset)
        o = _normalised(acc_ref)
        od = o[:, :tq] - lam * o[:, tq:]
        ms = jnp.mean(od * od, axis=0, keepdims=True)
        y = (od * lax.rsqrt(ms + RMS_EPS)).T * subln_ref[...] * (1.0 - lam_init)
        o_ref[:, cols] = y.astype(o_ref.dtype)


def _attn_b(lq1, lk1, lq2, lk2, subln, qd, kd, vd, batch, t, lam_init):
    tq = min(256, t)
    tk = min(512, t)
    nq = t // tq
    w = qd.shape[1]
    small = lambda a: pl.BlockSpec((1, a.shape[-1]), lambda b, i: (0, 0))
    vecs = [a.reshape(1, -1) for a in (lq1, lk1, lq2, lk2, subln)]
    return pl.pallas_call(
        functools.partial(_attn_b_kernel, lam_init=lam_init, t=t, tq=tq, tk=tk),
        grid=(batch, nq),
        in_specs=[small(a) for a in vecs]
        + [pl.BlockSpec((tq, w), lambda b, i: (b * nq + i, 0)),
           pl.BlockSpec((t, w), lambda b, i: (b, 0)),
           pl.BlockSpec((t, w), lambda b, i: (b, 0))],
        out_specs=pl.BlockSpec((tq, w), lambda b, i: (b * nq + i, 0)),
        out_shape=jax.ShapeDtypeStruct(qd.shape, BF16),
        scratch_shapes=_attn_scratch(2 * tq, w, t, tk)
        + [pltpu.VMEM((B_HEADS, 2 + tk // tq, tk, tq), F32)],
        compiler_params=_params(("parallel", "arbitrary")),
        name="attn_b",
    )(*vecs, qd, kd, vd)


def _out_proj_kernel(x_ref, a1_ref, a2_ref, w1_ref, w2_ref, o_ref):
    o_ref[...] = x_ref[...] + _dot(a1_ref[...], w1_ref[...]) + _dot(a2_ref[...], w2_ref[...])


def _out_proj(x, a1, a2, w1, w2):
    n, d = x.shape
    tm = min(ROW_TILE, n)
    return pl.pallas_call(
        _out_proj_kernel,
        grid=(n // tm,),
        in_specs=[pl.BlockSpec((tm, d), lambda i: (i, 0)),
                  pl.BlockSpec((tm, a1.shape[1]), lambda i: (i, 0)),
                  pl.BlockSpec((tm, a2.shape[1]), lambda i: (i, 0)),
                  pl.BlockSpec(w1.shape, lambda i: (0, 0)),
                  pl.BlockSpec(w2.shape, lambda i: (0, 0))],
        out_specs=pl.BlockSpec((tm, d), lambda i: (i, 0)),
        out_shape=jax.ShapeDtypeStruct((n, d), F32),
        compiler_params=_params(("parallel",)),
        name="out_proj",
    )(x, a1, a2, w1, w2)


def _ffn_kernel(x_ref, g_ref, wg_ref, wu_ref, wd_ref, o_ref, acc_ref, *, nchunks, fc):
    x = x_ref[...]
    xn = _rms(x, g_ref[...]).astype(BF16)
    acc_ref[...] = x

    def body(c, carry):
        hg = _dot(xn, wg_ref[c])
        hu = _dot(xn, wu_ref[c])
        act = (hg / (1.0 + jnp.exp(-hg)) * hu).astype(BF16)
        acc_ref[...] += _dot(act, wd_ref[pl.ds(pl.multiple_of(c * fc, fc), fc), :])
        return carry

    lax.fori_loop(0, nchunks, body, 0)
    o_ref[...] = acc_ref[...]


def _ffn(x, g, wg3, wu3, wd):
    n, d = x.shape
    nchunks, _, fc = wg3.shape
    tm = min(ROW_TILE, n)
    return pl.pallas_call(
        functools.partial(_ffn_kernel, nchunks=nchunks, fc=fc),
        grid=(n // tm,),
        in_specs=[pl.BlockSpec((tm, d), lambda i: (i, 0)),
                  pl.BlockSpec((1, d), lambda i: (0, 0)),
                  _const_spec(wg3), _const_spec(wu3), _const_spec(wd)],
        out_specs=pl.BlockSpec((tm, d), lambda i: (i, 0)),
        out_shape=jax.ShapeDtypeStruct((n, d), F32),
        scratch_shapes=[pltpu.VMEM((tm, d), F32)],
        compiler_params=_params(("parallel",)),
        name="ffn",
    )(x, g.reshape(1, d), wg3, wu3, wd)


def _cd_proj_kernel(x_ref, g_ref, w_ref, qn_ref, kvn_ref, wq1_ref, wq2_ref, wk_ref, wv_ref,
                    cos_ref, sin_ref, qc_ref, kc_ref, vc_ref, qd_ref, kd_ref, vd_ref):
    h = _rms(x_ref[...], g_ref[...]).astype(BF16)
    offs = np.cumsum((0,) + CD_PROJ_WIDTHS)
    seg = lambda k: _dot(h, w_ref[:, int(offs[k]):int(offs[k + 1])])
    cq, ckv, kr_plain, kr_rot = seg(0), seg(1), seg(2), seg(3)
    qd_ref[...] = (seg(4) * (HEAD_DIM ** -0.5 * LOG2E)).astype(BF16)
    kd_ref[...] = seg(5).astype(BF16)
    vd_ref[...] = seg(6).astype(BF16)
    cqn = _rms(cq, qn_ref[...]).astype(BF16)
    ckvn = _rms(ckv, kvn_ref[...]).astype(BF16)
    cos = cos_ref[...]
    sin = sin_ref[...]
    k_rope = kr_plain * cos + kr_rot * sin
    q1 = _dot(cqn, wq1_ref[...])
    q2 = _dot(cqn, wq2_ref[...])
    kn = _dot(ckvn, wk_ref[...])
    scale = (C_NOPE + C_ROPE) ** -0.5 * LOG2E
    for hh in range(C_HEADS):
        cols = slice(LANES * hh, LANES * (hh + 1))
        qc_ref[:, cols] = ((q1[:, cols] * cos + q2[:, cols] * sin) * scale).astype(BF16)
        kc_ref[:, cols] = (kn[:, cols] + k_rope).astype(BF16)
    vc_ref[...] = _dot(ckvn, wv_ref[...]).astype(BF16)


def _cd_proj(x, g, w, qn, kvn, wq1, wq2, wk, wv, cos_t, sin_t, t):
    n, d = x.shape
    tm = min(ROW_TILE, t)
    nt = t // tm
    full = _const_spec
    consts = [g.reshape(1, d), w, qn.reshape(1, -1), kvn.reshape(1, -1), wq1, wq2, wk, wv]
    out_w = (C_HEADS * LANES, C_HEADS * LANES, C_HEADS * C_V) + CD_PROJ_WIDTHS[4:]
    return pl.pallas_call(
        _cd_proj_kernel,
        grid=(n // tm,),
        in_specs=[pl.BlockSpec((tm, d), lambda i: (i, 0))] + [full(a) for a in consts]
        + [pl.BlockSpec((tm, LANES), lambda i: (i % nt, 0)),
           pl.BlockSpec((tm, LANES), lambda i: (i % nt, 0))],
        out_specs=[pl.BlockSpec((tm, wd), lambda i: (i, 0)) for wd in out_w],
        out_shape=[jax.ShapeDtypeStruct((n, wd), BF16) for wd in out_w],
        compiler_params=_params(("parallel",)),
        name="cd_proj",
    )(x, *consts, cos_t, sin_t)


def _pair_output(o, tq, hd):
    return jnp.concatenate([o[:hd, :tq], o[hd:, tq:]], axis=0).T


def _attn_c_kernel(q_ref, k_ref, v_ref, o_ref, vt_ref, *rest, t, tq, tk):
    refs, l_ref = rest[:-1], rest[-1]
    acc_ref = refs[1]

    @pl.when(pl.program_id(2) == 0)
    def _():
        _fill_vt(v_ref, vt_ref, tk, LANES)

    q0 = q_ref[:, :LANES]
    q1 = q_ref[:, LANES:]
    nk = t // tk

    def scores(j):
        rows = slice(j * tk, (j + 1) * tk)
        return jnp.concatenate([_dot_nt(k_ref[rows, :LANES], q0), _dot_nt(k_ref[rows, LANES:], q1)],
                               axis=1)

    _attention_chunks(refs, 0, nk, scores, lambda j: vt_ref[j, :LANES, :], l_ref=l_ref)
    o_ref[...] = _pair_output(_normalised(acc_ref, l_ref), tq, C_V).astype(o_ref.dtype)


def _attn_c(qc, kc, vc, batch, t):
    tq = min(256, t)
    tk = min(1024, t)
    nq = t // tq
    pairs = C_HEADS // 2
    return pl.pallas_call(
        functools.partial(_attn_c_kernel, t=t, tq=tq, tk=tk),
        grid=(batch, pairs, nq),
        in_specs=[pl.BlockSpec((tq, 2 * LANES), lambda b, p, i: (b * nq + i, p)),
                  pl.BlockSpec((t, 2 * LANES), lambda b, p, i: (b, p)),
                  pl.BlockSpec((t, LANES), lambda b, p, i: (b, p))],
        out_specs=pl.BlockSpec((tq, LANES), lambda b, p, i: (b * nq + i, p)),
        out_shape=jax.ShapeDtypeStruct(vc.shape, BF16),
        scratch_shapes=_attn_scratch(2 * tq, LANES, t, tk) + [pltpu.VMEM((1, 2 * tq), F32)],
        compiler_params=_params(("parallel", "parallel", "arbitrary")),
        name="attn_c",
    )(qc, kc, vc)


D_REACH = max((w // (2 * dil)) * dil for w, dil in D_PATTERNS)


def _d_tile_range(tq, tk):
    return -((D_REACH + tk) // tq) + 1, (tq + D_REACH) // tq - 1


def _attn_d_kernel(slope_ref, q_ref, k_ref, v_ref, o_ref, vt_ref, *rest, t, tq, tk):
    refs, bm_ref = rest[:-1], rest[-1]
    acc_ref = refs[1]
    p = pl.program_id(0)
    b = pl.program_id(1)
    i = pl.program_id(2)
    nk = t // tk
    u_lo, u_hi = _d_tile_range(tq, tk)
    masked_tile = u_hi - u_lo + 1

    @pl.when((b == 0) & (i == 0))
    def _():
        s0 = slope_ref[2 * p] * LOG2E
        s1 = slope_ref[2 * p + 1] * LOG2E
        amb = (lax.broadcasted_iota(jnp.int32, (tk, tq), 0)
               - lax.broadcasted_iota(jnp.int32, (tk, tq), 1))
        for u in range(u_lo, u_hi + 1):
            d = amb + u * tq
            ad = jnp.abs(d)
            mult = jnp.zeros((tk, tq), F32)
            for w, dil in D_PATTERNS:
                hit = (ad <= (w // (2 * dil)) * dil) & ((d & (dil - 1)) == 0)
                mult = mult + jnp.where(hit, 1.0, 0.0)
            adf = ad.astype(F32)
            logm = jnp.log2(jnp.maximum(mult, 1.0))
            seen = mult > 0.0
            bm_ref[u - u_lo] = jnp.concatenate([jnp.where(seen, s0 * adf - logm, MASKED),
                                                jnp.where(seen, s1 * adf - logm, MASKED)], axis=1)
        bm_ref[masked_tile] = jnp.full((tk, 2 * tq), MASKED, F32)

    @pl.when(i == 0)
    def _():
        _fill_vt(v_ref, vt_ref, tk, LANES)

    q = q_ref[...]
    lane = lax.broadcasted_iota(jnp.int32, (tq, LANES), 1)
    low = lane < HEAD_DIM
    zero = jnp.zeros_like(q)
    qq = jnp.concatenate([jnp.where(low, q, zero), jnp.where(low, zero, q)], axis=0)
    nwin = min(nk, (tq + 2 * D_REACH + tk - 1) // tk + 1)
    c_first = jnp.clip((i * tq - D_REACH) // tk, 0, nk - nwin)

    def scores(c):
        u = c * (tk // tq) - i
        tile = jnp.where((u >= u_lo) & (u <= u_hi), u - u_lo, masked_tile)
        return _dot_nt(k_ref[pl.ds(pl.multiple_of(c * tk, tk), tk), :], qq) - bm_ref[tile]

    _attention_chunks(refs, c_first, nwin, scores, lambda c: vt_ref[c])
    o_ref[...] = _pair_output(_normalised(acc_ref), tq, HEAD_DIM).astype(o_ref.dtype)


def _attn_d(qd, kd, vd, batch, t):
    tq = min(256, t)
    tk = min(512, t)
    nq = t // tq
    pairs = D_HEADS // 2
    slopes = jnp.asarray(_alibi_slopes(D_HEADS), F32)
    u_lo, u_hi = _d_tile_range(tq, tk)
    return pl.pallas_call(
        functools.partial(_attn_d_kernel, t=t, tq=tq, tk=tk),
        grid=(pairs, batch, nq),
        in_specs=[pl.BlockSpec(memory_space=pltpu.SMEM),
                  pl.BlockSpec((tq, LANES), lambda p, b, i: (b * nq + i, p)),
                  pl.BlockSpec((t, LANES), lambda p, b, i: (b, p)),
                  pl.BlockSpec((t, LANES), lambda p, b, i: (b, p))],
        out_specs=pl.BlockSpec((tq, LANES), lambda p, b, i: (b * nq + i, p)),
        out_shape=jax.ShapeDtypeStruct(qd.shape, BF16),
        scratch_shapes=_attn_scratch(2 * tq, LANES, t, tk)
        + [pltpu.VMEM((u_hi - u_lo + 2, tk, 2 * tq), F32)],
        compiler_params=_params(("arbitrary", "arbitrary", "arbitrary")),
        name="attn_d",
    )(slopes, qd, kd, vd)


def _router_kernel(x_ref, g_ref, rt_ref, xn_ref, ids_ref, gates_ref):
    xn = _rms(x_ref[...], g_ref[...])
    xn_ref[...] = xn
    x_hi = xn.astype(BF16)
    x_lo = (xn - x_hi.astype(F32)).astype(BF16)
    rt = rt_ref[...]
    r_hi = rt.astype(BF16)
    r_lo = (rt - r_hi.astype(F32)).astype(BF16)
    logits = _dot_nt(r_hi, x_hi) + _dot_nt(r_hi, x_lo) + _dot_nt(r_lo, x_hi)
    eid = lax.broadcasted_iota(jnp.int32, logits.shape, 0)
    v1 = jnp.max(logits, axis=0, keepdims=True)
    i1 = jnp.min(jnp.where(logits == v1, eid, N_EXPERTS), axis=0, keepdims=True)
    rest = jnp.where(eid == i1, -jnp.inf, logits)
    v2 = jnp.max(rest, axis=0, keepdims=True)
    i2 = jnp.min(jnp.where(rest == v2, eid, N_EXPERTS), axis=0, keepdims=True)
    e2 = jnp.exp(v2 - v1)
    g1 = 1.0 / (1.0 + e2)
    ids_ref[...] = jnp.concatenate([i1, i2], axis=0)
    gates_ref[...] = jnp.concatenate([g1, e2 * g1], axis=0)


def _router(x, g, router_t):
    n, d = x.shape
    tm = min(ROW_TILE, n)
    return pl.pallas_call(
        _router_kernel,
        grid=(n // tm,),
        in_specs=[pl.BlockSpec((tm, d), lambda i: (i, 0)),
                  pl.BlockSpec((1, d), lambda i: (0, 0)),
                  pl.BlockSpec(router_t.shape, lambda i: (0, 0))],
        out_specs=[pl.BlockSpec((tm, d), lambda i: (i, 0)),
                   pl.BlockSpec((2, tm), lambda i: (0, i)),
                   pl.BlockSpec((2, tm), lambda i: (0, i))],
        out_shape=[jax.ShapeDtypeStruct((n, d), F32),
                   jax.ShapeDtypeStruct((2, n), jnp.int32),
                   jax.ShapeDtypeStruct((2, n), F32)],
        compiler_params=_params(("parallel",)),
        name="router",
    )(x, g.reshape(1, d), router_t)


def _moe_kernel(te_ref, nv_ref, idx_hbm, x_hbm, wg_ref, wu_ref, wd_ref, o_ref,
                idx_ref, xbuf_ref, xb_ref, acc_ref, isem, rsem, *, nf, per_step):
    ti = pl.program_id(0)
    f = pl.program_id(1)
    nv = nv_ref[0]
    live = ti < nv
    tm = xb_ref.shape[0]

    def idx_copy(tile, slot):
        return pltpu.make_async_copy(idx_hbm.at[pl.ds(pl.multiple_of(tile * tm, tm), tm)],
                                     idx_ref.at[slot], isem.at[slot])

    def row_copy(tok, r, slot):
        return pltpu.make_async_copy(x_hbm.at[pl.ds(tok, 1)], xbuf_ref.at[slot, pl.ds(r, 1)],
                                     rsem.at[slot])

    def wait_rows(slot):
        def row(r, carry):
            row_copy(0, 0, slot).wait()
            return carry
        lax.fori_loop(0, nf * per_step, row, 0, unroll=8)

    @pl.when((ti == 0) & (f == 0))
    def _():
        idx_copy(0, 0).start()
        idx_copy(0, 0).wait()

        def row(r, carry):
            row_copy(idx_ref[0, jnp.minimum(r, tm - 1)], r, 0).start()
            return carry
        lax.fori_loop(0, nf * per_step, row, 0, unroll=8)

        @pl.when(1 < nv)
        def _():
            idx_copy(1, 1).start()

    for slot in range(2):
        @pl.when(live & (f == 0) & (ti % 2 == slot))
        def _(slot=slot):
            wait_rows(slot)

            @pl.when(ti + 1 < nv)
            def _():
                idx_copy(ti + 1, 1 - slot).wait()

            @pl.when(ti + 2 < nv)
            def _():
                idx_copy(ti + 2, slot).start()

            xb_ref[...] = xbuf_ref[slot, :tm, :].astype(BF16)
            acc_ref[...] = jnp.zeros(acc_ref.shape, F32)

    @pl.when(live)
    def _():
        nxt = jnp.minimum(ti + 1, nv - 1)
        islot = nxt % 2
        nslot = 1 - ti % 2
        for k in range(per_step):
            r = f * per_step + k
            rr = jnp.minimum(r, tm - 1)
            row_copy(idx_ref[islot, rr], r, nslot).start()
        xb = xb_ref[...]
        hg = _dot(xb, wg_ref[...])
        hu = _dot(xb, wu_ref[...])
        act = (hg / (1.0 + jnp.exp(-hg)) * hu).astype(BF16)
        acc_ref[...] += _dot(act, wd_ref[...])

    for slot in range(2):
        @pl.when((ti == nv - 1) & (f == nf - 1) & (ti % 2 == slot))
        def _(slot=slot):
            wait_rows(1 - slot)

    @pl.when(f == nf - 1)
    def _():
        o_ref[...] = jnp.where(live, acc_ref[...], 0.0)


def _moe(tile_expert, n_valid, src_tok, xn, wg, wu, wd):
    p = src_tok.shape[0]
    d = xn.shape[1]
    tm = MOE_TILE
    fc = MOE_FCHUNK
    nf = wg.shape[2] // fc
    per_step = -(-tm // nf)

    def tile(ti, nv):
        return jnp.minimum(ti, nv[0] - 1)

    grid_spec = pltpu.PrefetchScalarGridSpec(
        num_scalar_prefetch=2,
        grid=(p // tm, nf),
        in_specs=[pl.BlockSpec(memory_space=pl.ANY), pl.BlockSpec(memory_space=pl.ANY),
                  pl.BlockSpec((None, d, fc), lambda ti, f, te, nv: (te[tile(ti, nv)], 0, jnp.where(ti < nv[0], f, nf - 1))),
                  pl.BlockSpec((None, d, fc), lambda ti, f, te, nv: (te[tile(ti, nv)], 0, jnp.where(ti < nv[0], f, nf - 1))),
                  pl.BlockSpec((None, fc, d), lambda ti, f, te, nv: (te[tile(ti, nv)], jnp.where(ti < nv[0], f, nf - 1), 0))],
        out_specs=pl.BlockSpec((tm, d), lambda ti, f, te, nv: (ti, 0)),
        scratch_shapes=[pltpu.SMEM((2, tm), jnp.int32),
                        pltpu.VMEM((2, -(-nf * per_step // 8) * 8, d), F32), pltpu.VMEM((tm, d), BF16), pltpu.VMEM((tm, d), F32),
                        pltpu.SemaphoreType.DMA((2,)), pltpu.SemaphoreType.DMA((2,))],
    )
    return pl.pallas_call(
        functools.partial(_moe_kernel, nf=nf, per_step=per_step),
        grid_spec=grid_spec,
        out_shape=jax.ShapeDtypeStruct((p, d), F32),
        compiler_params=_params(("arbitrary", "arbitrary")),
        name="moe_experts",
    )(tile_expert, n_valid, src_tok, xn, wg, wu, wd)


def _combine_kernel(pos_hbm, x_ref, gate_ref, g_ref, ys_hbm, o_ref, idx_a, idx_b, buf_ref, isem, rsem,
                    *, rt, step0, nsteps):
    s = pl.program_id(0)
    idx_slots = (idx_a, idx_b)

    def idx_copy(step, slot):
        start = pl.multiple_of((step + step0) * 2 * rt, 2 * rt)
        return pltpu.make_async_copy(pos_hbm.at[pl.ds(start, 2 * rt)], idx_slots[slot], isem.at[slot])

    def row_copy(src, k, r, slot):
        return pltpu.make_async_copy(ys_hbm.at[pl.ds(src, 1)], buf_ref.at[slot, k, pl.ds(r, 1)],
                                     rsem.at[slot])

    def issue_rows(slot):
        def row(j, carry):
            for k in range(2):
                row_copy(idx_slots[slot][k * rt + j], k, j, slot).start()
            return carry
        lax.fori_loop(0, rt, row, 0, unroll=8)

    def wait_rows(slot):
        def row(j, carry):
            for k in range(2):
                row_copy(0, k, 0, slot).wait()
            return carry
        lax.fori_loop(0, rt, row, 0, unroll=8)

    @pl.when(s == 0)
    def _():
        idx_copy(0, 0).start()
        idx_copy(0, 0).wait()
        issue_rows(0)
        if nsteps > 1:
            idx_copy(1, 1).start()

    def step(slot):
        @pl.when(s + 1 < nsteps)
        def _():
            idx_copy(s + 1, 1 - slot).wait()
            issue_rows(1 - slot)

        @pl.when(s + 2 < nsteps)
        def _():
            idx_copy(s + 2, slot).start()

        wait_rows(slot)
        gates = gate_ref[...]
        y = x_ref[...] + gates[:, 0:1] * buf_ref[slot, 0] + gates[:, 1:2] * buf_ref[slot, 1]
        o_ref[...] = _rms(y, g_ref[...])

    for slot in range(2):
        @pl.when(s % 2 == slot)
        def _(slot=slot):
            step(slot)


def _combine(pos_flat, x, gates_t, g, ys, row0, nrows):
    d = x.shape[1]
    rt = GATHER_CHUNK // 2
    step0 = row0 // rt
    nsteps = nrows // rt
    return pl.pallas_call(
        functools.partial(_combine_kernel, rt=rt, step0=step0, nsteps=nsteps),
        grid=(nsteps,),
        in_specs=[pl.BlockSpec(memory_space=pl.ANY),
                  pl.BlockSpec((rt, d), lambda i: (i + step0, 0)),
                  pl.BlockSpec((rt, 2), lambda i: (i + step0, 0)),
                  pl.BlockSpec((1, d), lambda i: (0, 0)),
                  pl.BlockSpec(memory_space=pl.ANY)],
        out_specs=pl.BlockSpec((rt, d), lambda i: (i, 0)),
        out_shape=jax.ShapeDtypeStruct((nrows, d), F32),
        scratch_shapes=[pltpu.SMEM((2 * rt,), jnp.int32), pltpu.SMEM((2 * rt,), jnp.int32),
                        pltpu.VMEM((2, 2, rt, d), F32),
                        pltpu.SemaphoreType.DMA((2,)), pltpu.SemaphoreType.DMA((2,))],
        compiler_params=_params(("arbitrary",)),
        name="combine",
    )(pos_flat, x, gates_t, g.reshape(1, d), ys)


def _pad_cols(w, lo, total):
    return jnp.pad(w, ((0, 0), (lo, total - lo - w.shape[1])))


def _rot_half_cols(w):
    half = w.shape[1] // 2
    return jnp.concatenate([-w[:, half:], w[:, :half]], axis=1)


def _a_head_perm():
    half = A_HEADS // 2
    cols = []
    for p in range(half):
        cols += list(range(HEAD_DIM * p, HEAD_DIM * (p + 1)))
        cols += list(range(HEAD_DIM * (p + half), HEAD_DIM * (p + half + 1)))
    return np.asarray(cols, np.int32)


def _rope_tables(t):
    half = C_ROPE // 2
    inv = (ROPE_THETA ** (-np.arange(half) * 2.0 / C_ROPE)).astype(np.float32).astype(np.float64)
    ang = np.arange(t, dtype=np.float64)[:, None] * inv[None, :]
    cos = np.concatenate([np.cos(ang), np.cos(ang)], axis=1)
    sin = np.concatenate([np.sin(ang), np.sin(ang)], axis=1)
    pad = LANES - C_NOPE - C_ROPE
    cos_t = np.concatenate([np.ones((t, C_NOPE)), cos, np.zeros((t, pad))], axis=1)
    sin_t = np.concatenate([np.zeros((t, C_NOPE)), sin, np.zeros((t, pad))], axis=1)
    return jnp.asarray(cos_t, F32), jnp.asarray(sin_t, F32)


def _layer_ab(x, batch, t, norm_attn, w_in, sink, lq1, lk1, lq2, lk2, subln, w_out, norm_ffn,
              w_gate, w_up, w_down, lam_init):
    perm = _a_head_perm()
    qa_w = A_HEADS * HEAD_DIM
    w_in_k = jnp.concatenate([w_in[:, :qa_w][:, perm], w_in[:, qa_w:]], axis=1).astype(BF16)
    scale = HEAD_DIM ** -0.5
    qa, ka, va, qd, kd, vd = _norm_proj(x, norm_attn, w_in_k, AB_WIDTHS,
                                        (scale * LOG2E, 1.0, 1.0, scale * LOG2E, 1.0, 1.0))
    oa = _attn_a(sink, qa, ka, va, batch, t)
    od = _attn_b(lq1, lk1, lq2, lk2, subln, qd, kd, vd, batch, t, lam_init)
    x = _out_proj(x, oa, od, w_out[:qa_w][perm].astype(BF16), w_out[qa_w:].astype(BF16))
    nchunks = w_gate.shape[1] // FFN_CHUNK
    to_chunks = lambda w: w.reshape(w.shape[0], nchunks, FFN_CHUNK).transpose(1, 0, 2).astype(BF16)
    return _ffn(x, norm_ffn, to_chunks(w_gate), to_chunks(w_up), w_down.astype(BF16))


def _cd_weights(w_in, w_uq, w_ukv):
    o = np.cumsum((0,) + CD_WIDTHS)
    w_kr = w_in[:, o[2]:o[3]]
    w_proj = jnp.concatenate([w_in[:, :o[2]], _pad_cols(w_kr, C_NOPE, LANES),
                              _pad_cols(_rot_half_cols(w_kr), C_NOPE, LANES), w_in[:, o[3]:]], axis=1)
    dq = C_NOPE + C_ROPE
    wq1, wq2, wk, wv = [], [], [], []
    for h in range(C_HEADS):
        wq_h = w_uq[:, dq * h:dq * (h + 1)]
        wq1.append(_pad_cols(wq_h, 0, LANES))
        wq2.append(_pad_cols(_rot_half_cols(wq_h[:, C_NOPE:]), C_NOPE, LANES))
        wkv_h = w_ukv[:, (C_NOPE + C_V) * h:(C_NOPE + C_V) * (h + 1)]
        wk.append(_pad_cols(wkv_h[:, :C_NOPE], 0, LANES))
        wv.append(wkv_h[:, C_NOPE:])
    cat = lambda ws: jnp.concatenate(ws, axis=1).astype(BF16)
    return w_proj.astype(BF16), cat(wq1), cat(wq2), cat(wk), cat(wv)


def _route(ids, n):
    e = ids.reshape(-1)
    onehot = (e[:, None] == jnp.arange(N_EXPERTS, dtype=jnp.int32)[None, :]).astype(jnp.int32)
    csum = jnp.cumsum(onehot, axis=0)
    rank = jnp.sum(onehot * (csum - 1), axis=1)
    counts = csum[-1]
    padded = ((counts + MOE_TILE - 1) // MOE_TILE) * MOE_TILE
    gend = jnp.cumsum(padded)
    gstart = gend - padded
    slot = gstart[e] + rank
    p_total = 2 * n + N_EXPERTS * MOE_TILE
    tok = jnp.tile(jnp.arange(n, dtype=jnp.int32), 2)
    src_tok = jnp.zeros((p_total,), jnp.int32).at[slot].set(tok, unique_indices=True)
    ntiles = p_total // MOE_TILE
    tile_start = jnp.arange(ntiles, dtype=jnp.int32) * MOE_TILE
    tile_expert = jnp.minimum(jnp.sum((tile_start[:, None] >= gend[None, :]).astype(jnp.int32), axis=1),
                              N_EXPERTS - 1).astype(jnp.int32)
    n_valid = (gend[-1] // MOE_TILE).astype(jnp.int32).reshape(1)
    return slot.astype(jnp.int32), src_tok, tile_expert, n_valid


def _layer_cd(x, batch, t, norm_attn, w_in, q_norm, kv_norm, w_uq, w_ukv, w_out, norm_ffn, router,
              w_gate, w_up, w_down, final_norm, splits):
    n = x.shape[0]
    w_proj, wq1, wq2, wk, wv = _cd_weights(w_in, w_uq, w_ukv)
    cos_t, sin_t = _rope_tables(t)
    qc, kc, vc, qd, kd, vd = _cd_proj(x, norm_attn, w_proj, q_norm, kv_norm, wq1, wq2, wk, wv,
                                      cos_t, sin_t, t)
    oc = _attn_c(qc, kc, vc, batch, t)
    od = _attn_d(qd, kd, vd, batch, t)
    cw = C_HEADS * C_V
    x = _out_proj(x, oc, od, w_out[:cw].astype(BF16), w_out[cw:].astype(BF16))
    xn, ids, gates = _router(x, norm_ffn, router.T)
    slot, src_tok, tile_expert, n_valid = _route(ids, n)
    ys = _moe(tile_expert, n_valid, src_tok, xn, w_gate.astype(BF16), w_up.astype(BF16),
              w_down.astype(BF16))
    rt = GATHER_CHUNK // 2
    pos_flat = slot.reshape(2, n // rt, rt).transpose(1, 0, 2).reshape(-1)
    gates_t = gates.T
    outs = []
    row0 = 0
    for rows in splits:
        outs.append(_combine(pos_flat, x, gates_t, final_norm, ys, row0, rows))
        row0 += rows
    return outs


def kernel(x_prompt, x_sample, ab_norm_attn, ab_w_in, ab_sink, ab_lambda_q1, ab_lambda_k1, ab_lambda_q2, ab_lambda_k2, ab_subln, ab_w_out, ab_norm_ffn, ffn_w_gate, ffn_w_up, ffn_w_down, cd_norm_attn, cd_w_in, cd_q_norm, cd_kv_norm, cd_w_uq, cd_w_ukv, cd_w_out, cd_norm_ffn, moe_router, moe_w_gate, moe_w_up, moe_w_down, final_norm):
    bp, t, d = x_prompt.shape
    bs = x_sample.shape[0]
    assert x_sample.shape[1] == t
    batch = bp + bs
    x = jnp.concatenate([x_prompt.reshape(bp * t, d), x_sample.reshape(bs * t, d)], axis=0)
    lam_init = 0.8 - 0.6 * math.exp(-0.3 * 0)
    x = _layer_ab(x, batch, t, ab_norm_attn[0], ab_w_in[0], ab_sink[0], ab_lambda_q1[0],
                  ab_lambda_k1[0], ab_lambda_q2[0], ab_lambda_k2[0], ab_subln[0], ab_w_out[0],
                  ab_norm_ffn[0], ffn_w_gate[0], ffn_w_up[0], ffn_w_down[0], lam_init)
    y_prompt, y_sample = _layer_cd(x, batch, t, cd_norm_attn[0], cd_w_in[0], cd_q_norm[0],
                                   cd_kv_norm[0], cd_w_uq[0], cd_w_ukv[0], cd_w_out[0],
                                   cd_norm_ffn[0], moe_router[0], moe_w_gate[0], moe_w_up[0],
                                   moe_w_down[0], final_norm, (bp * t, bs * t))
    return (y_prompt.reshape(bp, t, d), y_sample.reshape(bs, t, d))
```
